```python
import math
import jax, jax.numpy as jnp
from jax import lax
import numpy as np

D_MODEL = 1024
BATCH = 16
SEQ = 2048
DEPTH = 4
DEC_BATCH = 8
DEC_SEQ = 4096
PAST_LEN = 128

GRID_W = 64
HEAD_DIM = 64
RMS_EPS = 1e-6
NEG_INF = -1e30

NA_HEADS = 8
NA_WIN_ROWS = 8
NA_WIN_COLS = 16
NA_COL_BLOCK = 16
NA_KEY_COLS = NA_COL_BLOCK + NA_WIN_COLS

DIFF_HEADS = 4
DIFF_Q_BLOCK = 128

DIL_PAIRS = ((128, 1), (512, 4), (2048, 16))
DIL_GROUPS = 3
DIL_HEADS = 4
DIL_Q_BLOCK = 64

A_W = NA_HEADS * HEAD_DIM
B_QK = DIFF_HEADS * 2 * HEAD_DIM
B_W = DIFF_HEADS * 2 * HEAD_DIM
C_QKV = DIL_GROUPS * DIL_HEADS * HEAD_DIM
C_W = DIL_HEADS * HEAD_DIM
N_BRANCH = 3
IN_SPLITS = (A_W, A_W, A_W, A_W, B_QK, B_QK, B_W, B_W, C_QKV, C_QKV, C_QKV, C_W, N_BRANCH * D_MODEL)
IN_WIDTH = 4 * A_W + 2 * B_QK + 2 * B_W + 3 * C_QKV + C_W + N_BRANCH * D_MODEL

kernel_name = 'hybrid_natten_diff_dilated_encoder'


def rms_norm(x, g):
    xf = x.astype(jnp.float32)
    y = xf * lax.rsqrt(jnp.mean(xf * xf, axis=-1, keepdims=True) + RMS_EPS)
    return (y * g.astype(jnp.float32)).astype(x.dtype)


def alibi_slopes(n):
    return jnp.asarray(2.0 ** (-8.0 * np.arange(1, n + 1) / n), dtype=jnp.float32)


def neighborhood_attention(q, k, v, rpb):
    b, l, h, dh = q.shape
    rows = l // GRID_W
    kh = min(NA_WIN_ROWS, rows)
    n_cb = GRID_W // NA_COL_BLOCK
    qcol = np.arange(GRID_W).reshape(n_cb, NA_COL_BLOCK)
    qstart = np.clip(qcol - NA_WIN_COLS // 2, 0, GRID_W - NA_WIN_COLS)
    kstart = np.clip(np.arange(n_cb) * NA_COL_BLOCK - NA_WIN_COLS // 2, 0, GRID_W - NA_KEY_COLS)
    kcol = kstart[:, None] + np.arange(NA_KEY_COLS)
    col_ok = (kcol[:, None, :] >= qstart[:, :, None]) & (kcol[:, None, :] < qstart[:, :, None] + NA_WIN_COLS)
    col_idx = np.clip(kcol[:, None, :] - qcol[:, :, None] + NA_WIN_COLS - 1, 0, 2 * NA_WIN_COLS - 2)
    mask = jnp.asarray(np.broadcast_to(col_ok[:, None, :, None, :], (n_cb, 1, NA_COL_BLOCK, kh, NA_KEY_COLS)).reshape(n_cb, 1, NA_COL_BLOCK, kh * NA_KEY_COLS))
    row_start = np.clip(np.arange(rows) - kh // 2, 0, rows - kh).astype(np.int32)
    qg = (q * dh ** -0.5).reshape(b, rows, n_cb, NA_COL_BLOCK, h, dh)
    kg = k.reshape(b, rows, GRID_W, h, dh)
    vg = v.reshape(b, rows, GRID_W, h, dh)
    rpb32 = rpb.astype(jnp.float32)

    def one_row(args):
        r, rs = args

        def gather_keys(x):
            x = lax.dynamic_slice_in_dim(x, rs, kh, axis=1)[:, :, kcol]
            return jnp.moveaxis(x, 1, 2).reshape(b, n_cb, kh * NA_KEY_COLS, h, dh)

        kb = gather_keys(kg)
        vb = gather_keys(vg)
        qr = lax.dynamic_index_in_dim(qg, r, axis=1, keepdims=False)
        s = jnp.einsum('bcqhe,bckhe->bchqk', qr, kb).astype(jnp.float32)
        row_off = rs + jnp.arange(kh, dtype=jnp.int32) - r
        bias = rpb32[:, row_off + NA_WIN_ROWS - 1][:, :, col_idx]
        bias = jnp.transpose(bias, (2, 0, 3, 1, 4)).reshape(n_cb, h, NA_COL_BLOCK, kh * NA_KEY_COLS)
        p = jax.nn.softmax(jnp.where(mask, s + bias, NEG_INF), axis=-1).astype(v.dtype)
        return jnp.einsum('bchqk,bckhe->bcqhe', p, vb)

    out = lax.map(one_row, (jnp.arange(rows, dtype=jnp.int32), jnp.asarray(row_start)))
    return jnp.moveaxis(out, 0, 1).reshape(b, l, h * dh)


def diff_attention(q, k, v, lam, slopes, g_diff, lam_init):
    b, l, h, _, dh = q.shape
    nq = l // DIFF_Q_BLOCK
    qb = jnp.moveaxis((q * dh ** -0.5).reshape(b, nq, DIFF_Q_BLOCK, h, 2, dh), 1, 0)
    kpos = jnp.arange(l, dtype=jnp.int32)

    def one_block(args):
        i, qi = args
        s = jnp.einsum('bqhme,bkhme->bhmqk', qi, k).astype(jnp.float32)
        qpos = i * DIFF_Q_BLOCK + jnp.arange(DIFF_Q_BLOCK, dtype=jnp.int32)
        dist = jnp.abs(qpos[:, None] - kpos[None, :]).astype(jnp.float32)
        a = jax.nn.softmax(s - slopes[None, :, None, None, None] * dist, axis=-1)
        w = (a[:, :, 0] - lam * a[:, :, 1]).astype(v.dtype)
        return jnp.einsum('bhqk,bkhe->bqhe', w, v)

    o = lax.map(one_block, (jnp.arange(nq, dtype=jnp.int32), qb))
    o = jnp.moveaxis(o, 0, 1).reshape(b, l, h, 2 * dh)
    o = rms_norm(o, g_diff) * (1.0 - lam_init)
    return o.reshape(b, l, h * 2 * dh)


def dilated_group_attention(q, k, v, window, dilation, slopes):
    b, l, h, dh = q.shape
    n = window // (2 * dilation)
    m = l // dilation
    nb = -(-m // DIL_Q_BLOCK)
    mp = nb * DIL_Q_BLOCK
    span = DIL_Q_BLOCK + 2 * n

    def to_classes(x):
        return jnp.moveaxis(x.reshape(b, m, dilation, h, dh), 2, 1)

    qc = jnp.pad(to_classes(q * dh ** -0.5), ((0, 0), (0, 0), (0, mp - m), (0, 0), (0, 0)))
    qc = qc.reshape(b, dilation, nb, DIL_Q_BLOCK, h, dh)
    pad_k = ((0, 0), (0, 0), (n, mp - m + n), (0, 0), (0, 0))
    idx = np.arange(nb)[:, None] * DIL_Q_BLOCK + np.arange(span)[None, :]
    kb = jnp.pad(to_classes(k), pad_k)[:, :, idx]
    vb = jnp.pad(to_classes(v), pad_k)[:, :, idx]
    s = jnp.einsum('bdnqhe,bdnkhe->bdnhqk', qc, kb).astype(jnp.float32)
    rel = (np.arange(span)[None, :] - n) - np.arange(DIL_Q_BLOCK)[:, None]
    mk = idx - n
    ok = (np.abs(rel)[None] <= n) & (mk[:, None, :] >= 0) & (mk[:, None, :] < m)
    bias = -slopes[:, None, None] * jnp.asarray(dilation * np.abs(rel), dtype=jnp.float32)
    s = jnp.where(jnp.asarray(ok)[None, None, :, None], s + bias[None, None, None], NEG_INF)
    lse = jax.nn.logsumexp(s, axis=-1)
    p = jnp.exp(s - lse[..., None]).astype(v.dtype)
    o = jnp.einsum('bdnhqk,bdnkhe->bdnqhe', p, vb).reshape(b, dilation, mp, h, dh)[:, :, :m]
    o = jnp.moveaxis(o, 1, 2).reshape(b, l, h, dh)
    lse = jnp.moveaxis(lse, -1, -2).reshape(b, dilation, mp, h)[:, :, :m]
    lse = jnp.moveaxis(lse, 1, 2).reshape(b, l, h)
    return o, lse


def dilated_attention(q, k, v, slopes):
    b, l, _, dh = q.shape
    outs, lses = [], []
    for g, (window, dilation) in enumerate(DIL_PAIRS):
        sl = slice(g * DIL_HEADS, (g + 1) * DIL_HEADS)
        o, lse = dilated_group_attention(q[:, :, sl], k[:, :, sl], v[:, :, sl], window, dilation, slopes[sl])
        outs.append(o)
        lses.append(lse)
    wts = jax.nn.softmax(jnp.stack(lses, axis=0), axis=0)
    o = jnp.sum(wts[..., None] * jnp.stack(outs, axis=0).astype(jnp.float32), axis=0)
    return o.astype(q.dtype).reshape(b, l, DIL_HEADS * dh)


def encoder_layer(x, lam_init, g_norm, w_in, b_gate, rpb, lam_qk, g_diff, w_br_a, w_br_b, w_br_c, w_out):
    b, t, _ = x.shape
    h = rms_norm(x, g_norm)
    u = h @ w_in
    offs = [int(o) for o in np.cumsum(IN_SPLITS)[:-1]]
    qa, ka, va, za, qb, kb, vb, zb, qc, kc, vc, zc, gl = jnp.split(u, offs, axis=-1)
    ya = neighborhood_attention(qa.reshape(b, t, NA_HEADS, HEAD_DIM), ka.reshape(b, t, NA_HEADS, HEAD_DIM),
                                va.reshape(b, t, NA_HEADS, HEAD_DIM), rpb)
    lq = lam_qk.astype(jnp.float32)
    lam = jnp.exp(jnp.sum(lq[0] * lq[1])) - jnp.exp(jnp.sum(lq[2] * lq[3])) + lam_init
    yb = diff_attention(qb.reshape(b, t, DIFF_HEADS, 2, HEAD_DIM), kb.reshape(b, t, DIFF_HEADS, 2, HEAD_DIM),
                        vb.reshape(b, t, DIFF_HEADS, 2 * HEAD_DIM), lam, alibi_slopes(DIFF_HEADS), g_diff, lam_init)
    nh_c = DIL_GROUPS * DIL_HEADS
    yc = dilated_attention(qc.reshape(b, t, nh_c, HEAD_DIM), kc.reshape(b, t, nh_c, HEAD_DIM),
                           vc.reshape(b, t, nh_c, HEAD_DIM), alibi_slopes(nh_c))
    pa = (ya * jax.nn.silu(za)) @ w_br_a
    pb = (yb * jax.nn.silu(zb)) @ w_br_b
    pc = (yc * jax.nn.silu(zc)) @ w_br_c
    gates = jax.nn.sigmoid((gl.reshape(b, t, N_BRANCH, D_MODEL) + b_gate).astype(jnp.float32)).astype(x.dtype)
    merged = gates[:, :, 0] * pa + gates[:, :, 1] * pb + gates[:, :, 2] * pc
    return x + merged @ w_out


def trunk(x, g_norm, w_in, b_gate, rpb, lam_qk, g_diff, w_br_a, w_br_b, w_br_c, w_out, g_final):
    for l in range(DEPTH):
        lam_init = 0.8 - 0.6 * math.exp(-0.3 * l)
        x = encoder_layer(x, lam_init, g_norm[l], w_in[l], b_gate[l], rpb[l], lam_qk[l], g_diff[l],
                          w_br_a[l], w_br_b[l], w_br_c[l], w_out[l])
    return rms_norm(x, g_final)


def setup_inputs(seed: int = 0) -> dict:
    key = jax.random.key(seed)
    ks = jax.random.split(key, 13)
    f32 = jnp.float32
    nrm = lambda k_, shape: jax.random.normal(k_, shape, dtype=f32)
    return {
        'x_prompt': nrm(ks[0], (BATCH, SEQ, D_MODEL)),
        'x_sample': nrm(ks[1], (DEC_BATCH, DEC_SEQ, D_MODEL)),
        'g_norm': 1.0 + 0.05 * nrm(ks[2], (DEPTH, D_MODEL)),
        'w_in': nrm(ks[3], (DEPTH, D_MODEL, IN_WIDTH)) * D_MODEL ** -0.5,
        'b_gate': 0.1 * nrm(ks[4], (DEPTH, N_BRANCH, D_MODEL)),
        'rpb': 0.1 * nrm(ks[5], (DEPTH, NA_HEADS, 2 * NA_WIN_ROWS - 1, 2 * NA_WIN_COLS - 1)),
        'lam_qk': 0.1 * nrm(ks[6], (DEPTH, 4, HEAD_DIM)),
        'g_diff': 1.0 + 0.05 * nrm(ks[7], (DEPTH, 2 * HEAD_DIM)),
        'w_br_a': nrm(ks[8], (DEPTH, A_W, D_MODEL)) * A_W ** -0.5,
        'w_br_b': nrm(ks[9], (DEPTH, B_W, D_MODEL)) * B_W ** -0.5,
        'w_br_c': nrm(ks[10], (DEPTH, C_W, D_MODEL)) * C_W ** -0.5,
        'w_out': nrm(ks[11], (DEPTH, D_MODEL, D_MODEL)) * D_MODEL ** -0.5,
        'g_final': 1.0 + 0.05 * nrm(ks[12], (D_MODEL,)),
    }


def reference(x_prompt, x_sample, g_norm, w_in, b_gate, rpb, lam_qk, g_diff, w_br_a, w_br_b, w_br_c, w_out, g_final):
    y_prompt = trunk(x_prompt, g_norm, w_in, b_gate, rpb, lam_qk, g_diff, w_br_a, w_br_b, w_br_c, w_out, g_final)
    y_sample = trunk(x_sample, g_norm, w_in, b_gate, rpb, lam_qk, g_diff, w_br_a, w_br_b, w_br_c, w_out, g_final)
    return (y_prompt, y_sample)
```

```python
import functools
import math

import numpy as np
import jax
import jax.numpy as jnp
from jax import lax
from jax.experimental import pallas as pl
from jax.experimental.pallas import tpu as pltpu

F32 = jnp.float32
BF16 = jnp.bfloat16

D_MODEL = 1024
DEPTH = 4
GRID_W = 64
HEAD_DIM = 64
RMS_EPS = 1e-6
NEG_INF = -1e30
LANES = 128

NA_HEADS = 8
NA_WIN_ROWS = 8
NA_WIN_COLS = 16
NA_Q_ROWS = 4
NA_K_ROWS = NA_Q_ROWS + NA_WIN_ROWS
NA_TQ = NA_Q_ROWS * GRID_W
NA_TK = NA_K_ROWS * GRID_W

DIFF_HEADS = 4
DIFF_TQ = 256
DIFF_TK = 512

DIL_PAIRS = ((128, 1), (512, 4), (2048, 16))
DIL_GROUPS = 3
DIL_HEADS = 4
DIL_SIDE = 64
DIL_TQ = 256

A_W = NA_HEADS * HEAD_DIM
B_W = DIFF_HEADS * 2 * HEAD_DIM
C_QKV = DIL_GROUPS * DIL_HEADS * HEAD_DIM
C_W = DIL_HEADS * HEAD_DIM
N_BRANCH = 3
GL_W = N_BRANCH * D_MODEL
IN_WIDTH = 4 * A_W + 4 * B_W + 3 * C_QKV + C_W + GL_W

OFF_GL = 0
OFF_QA = GL_W
OFF_KA = OFF_QA + A_W
OFF_VA = OFF_KA + A_W
OFF_ZA = OFF_VA + A_W
OFF_QB = OFF_ZA + A_W
OFF_KB = OFF_QB + B_W
OFF_VB = OFF_KB + B_W
OFF_ZB = OFF_VB + B_W
OFF_QC = OFF_ZB + B_W
OFF_KC = OFF_QC + C_QKV
OFF_VC = OFF_KC + C_QKV
OFF_ZC = OFF_VC + C_QKV
assert OFF_ZC + C_W == IN_WIDTH

VMEM_LIMIT = 56 * 1024 * 1024

IN_TM = 1024
IN_TN = IN_WIDTH // 4
MERGE_TM = 512

_NT = (((1,), (1,)), ((), ()))


def _params(*sem):
    return pltpu.CompilerParams(dimension_semantics=sem, vmem_limit_bytes=VMEM_LIMIT)


def _silu(z):
    return z / (1.0 + jnp.exp(-z))


def _head_masks(scale):
    lane = lax.broadcasted_iota(jnp.int32, (1, LANES), 1)
    lo = jnp.where(lane < HEAD_DIM, scale, 0.0).astype(BF16)
    hi = jnp.where(lane >= HEAD_DIM, scale, 0.0).astype(BF16)
    return lo, hi


def _inproj_kernel(x_ref, g_ref, w_ref, o_ref, h_ref):
    @pl.when(pl.program_id(1) == 0)
    def _():
        x = x_ref[...]
        ms = jnp.mean(x * x, axis=-1, keepdims=True)
        h_ref[...] = (x * lax.rsqrt(ms + RMS_EPS) * g_ref[...]).astype(BF16)

    o_ref[...] = jnp.dot(h_ref[...], w_ref[...], preferred_element_type=F32).astype(BF16)


def _inproj(x, g, w):
    t = x.shape[0]
    tm = min(IN_TM, t)
    return pl.pallas_call(
        _inproj_kernel,
        out_shape=jax.ShapeDtypeStruct((t, IN_WIDTH), BF16),
        grid=(t // tm, IN_WIDTH // IN_TN),
        in_specs=[
            pl.BlockSpec((tm, D_MODEL), lambda i, j: (i, 0)),
            pl.BlockSpec((1, D_MODEL), lambda i, j: (0, 0)),
            pl.BlockSpec((D_MODEL, IN_TN), lambda i, j: (0, j)),
        ],
        out_specs=pl.BlockSpec((tm, IN_TN), lambda i, j: (i, j)),
        scratch_shapes=[pltpu.VMEM((tm, D_MODEL), BF16)],
        compiler_params=_params("parallel", "arbitrary"),
        name="inproj",
    )(x, g, w)


def _natten_bias(rpb):
    ri = np.arange(NA_Q_ROWS)[:, None, None, None]
    c = np.arange(GRID_W)[None, :, None, None]
    kr = np.arange(NA_K_ROWS)[None, None, :, None]
    kc = np.arange(GRID_W)[None, None, None, :]
    qstart = np.clip(c - NA_WIN_COLS // 2, 0, GRID_W - NA_WIN_COLS)
    col_ok = (kc >= qstart) & (kc < qstart + NA_WIN_COLS)
    col_idx = np.clip(kc - c + NA_WIN_COLS - 1, 0, 2 * NA_WIN_COLS - 2)
    shape = (NA_Q_ROWS, GRID_W, NA_K_ROWS, GRID_W)
    out = []
    for variant, q_minus_k_row0 in enumerate((0, NA_WIN_ROWS // 2, NA_WIN_ROWS)):
        first = (0 * ri, ri, 0 * ri + NA_K_ROWS - NA_WIN_ROWS)[variant]
        row_ok = (kr >= first) & (kr < first + NA_WIN_ROWS)
        row_idx = np.clip(kr - q_minus_k_row0 - ri + NA_WIN_ROWS - 1, 0, 2 * NA_WIN_ROWS - 2)
        ok = np.broadcast_to(row_ok & col_ok, shape).reshape(NA_TQ, NA_TK)
        ridx = np.broadcast_to(row_idx, shape).reshape(NA_TQ, NA_TK)
        cidx = np.broadcast_to(col_idx, shape).reshape(NA_TQ, NA_TK)
        vals = rpb.astype(F32)[:, ridx, cidx]
        out.append(jnp.where(jnp.asarray(ok)[None], vals, NEG_INF))
    return jnp.stack(out, axis=0)


def _natten_kernel(bias_ref, q_ref, k_ref, v_ref, z_ref, o_ref, *, rows):
    i = pl.program_id(1)
    krow0 = jnp.clip(i * NA_Q_ROWS - NA_WIN_ROWS // 2, 0, rows - NA_K_ROWS)
    kstart = pl.multiple_of(krow0 * GRID_W, GRID_W)
    lo, hi = _head_masks(HEAD_DIM ** -0.5)
    lane = lax.broadcasted_iota(jnp.int32, (NA_TQ, LANES), 1)
    for p in range(NA_HEADS // 2):
        cols = slice(p * LANES, (p + 1) * LANES)
        q = q_ref[:, cols]
        k = k_ref[pl.ds(kstart, NA_TK), cols]
        v = v_ref[pl.ds(kstart, NA_TK), cols]
        halves = []
        for half, mask in enumerate((lo, hi)):
            s = lax.dot_general(q * mask, k, _NT, preferred_element_type=F32)
            s = s + bias_ref[0, 2 * p + half]
            m = jnp.max(s, axis=-1, keepdims=True)
            e = jnp.exp(s - m)
            l = jnp.sum(e, axis=-1, keepdims=True)
            o = jnp.dot(e.astype(BF16), v, preferred_element_type=F32)
            halves.append(o * (1.0 / l))
        y = jnp.where(lane < HEAD_DIM, halves[0], halves[1])
        o_ref[:, cols] = (y * _silu(z_ref[:, cols].astype(F32))).astype(BF16)


def _natten(u, bias, b, l):
    rows = l // GRID_W
    nblk = l // NA_TQ
    assert rows >= NA_K_ROWS and nblk >= 3

    def bias_map(bi, i):
        return (jnp.where(i == 0, 0, jnp.where(i == nblk - 1, 2, 1)), 0, 0, 0)

    return pl.pallas_call(
        functools.partial(_natten_kernel, rows=rows),
        out_shape=jax.ShapeDtypeStruct((b * l, A_W), BF16),
        grid=(b, nblk),
        in_specs=[
            pl.BlockSpec((1, NA_HEADS, NA_TQ, NA_TK), bias_map),
            pl.BlockSpec((NA_TQ, A_W), lambda bi, i: (bi * nblk + i, OFF_QA // A_W)),
            pl.BlockSpec((l, A_W), lambda bi, i: (bi, OFF_KA // A_W)),
            pl.BlockSpec((l, A_W), lambda bi, i: (bi, OFF_VA // A_W)),
            pl.BlockSpec((NA_TQ, A_W), lambda bi, i: (bi * nblk + i, OFF_ZA // A_W)),
        ],
        out_specs=pl.BlockSpec((NA_TQ, A_W), lambda bi, i: (bi * nblk + i, 0)),
        compiler_params=_params("parallel", "arbitrary"),
        name="natten",
    )(bias, u, u, u, u)


def _diff_kernel(lq_ref, gd_ref, q_ref, k_ref, v_ref, z_ref, o_ref, s_ref, *, l, tq, tk, lam_init):
    h = pl.program_id(1)
    i = pl.program_id(2)
    nck = l // tk
    lq = lq_ref[...]
    lam = (jnp.exp(jnp.sum(lq[0:1] * lq[1:2], axis=-1, keepdims=True))
           - jnp.exp(jnp.sum(lq[2:3] * lq[3:4], axis=-1, keepdims=True)) + lam_init)
    slope = jnp.where(h == 0, 4.0 ** -1, jnp.where(h == 1, 4.0 ** -2, jnp.where(h == 2, 4.0 ** -3, 4.0 ** -4)))
    lo, hi = _head_masks(HEAD_DIM ** -0.5)
    q = q_ref[...]
    q0 = q * lo
    q1 = q * hi
    q_minus_k = (i * tq + lax.broadcasted_iota(jnp.int32, (tq, tk), 0)
                 - lax.broadcasted_iota(jnp.int32, (tq, tk), 1))

    def scores(c, carry):
        m0, m1 = carry
        k = k_ref[pl.ds(pl.multiple_of(c * tk, tk), tk), :]
        bias = jnp.abs(q_minus_k - c * tk).astype(F32) * (-slope)
        s0 = lax.dot_general(q0, k, _NT, preferred_element_type=F32) + bias
        s1 = lax.dot_general(q1, k, _NT, preferred_element_type=F32) + bias
        s_ref[0, c] = s0
        s_ref[1, c] = s1
        return (jnp.maximum(m0, jnp.max(s0, axis=-1, keepdims=True)),
                jnp.maximum(m1, jnp.max(s1, axis=-1, keepdims=True)))

    neg = jnp.full((tq, 1), -jnp.inf, F32)
    m0, m1 = lax.fori_loop(0, nck, scores, (neg, neg))

    def exps(c, carry):
        l0, l1 = carry
        e0 = jnp.exp(s_ref[0, c] - m0)
        e1 = jnp.exp(s_ref[1, c] - m1)
        s_ref[0, c] = e0
        s_ref[1, c] = e1
        return (l0 + jnp.sum(e0, axis=-1, keepdims=True), l1 + jnp.sum(e1, axis=-1, keepdims=True))

    zero = jnp.zeros((tq, 1), F32)
    l0, l1 = lax.fori_loop(0, nck, exps, (zero, zero))
    a0 = 1.0 / l0
    a1 = lam / l1

    def weighted(c, acc):
        w = (s_ref[0, c] * a0 - s_ref[1, c] * a1).astype(BF16)
        v = v_ref[pl.ds(pl.multiple_of(c * tk, tk), tk), :]
        return acc + jnp.dot(w, v, preferred_element_type=F32)

    o = lax.fori_loop(0, nck, weighted, jnp.zeros((tq, LANES), F32))
    ms = jnp.mean(o * o, axis=-1, keepdims=True)
    y = (o * lax.rsqrt(ms + RMS_EPS) * gd_ref[...]) * (1.0 - lam_init)
    o_ref[...] = (y * _silu(z_ref[...].astype(F32))).astype(BF16)


def _diff(u, lam_qk, g_diff, b, l, lam_init):
    tq = min(DIFF_TQ, l)
    tk = min(DIFF_TK, l)
    nq = l // tq
    qb, kb, vb, zb = (off // LANES for off in (OFF_QB, OFF_KB, OFF_VB, OFF_ZB))
    return pl.pallas_call(
        functools.partial(_diff_kernel, l=l, tq=tq, tk=tk, lam_init=lam_init),
        out_shape=jax.ShapeDtypeStruct((b * l, B_W), BF16),
        grid=(b, DIFF_HEADS, nq),
        in_specs=[
            pl.BlockSpec((4, HEAD_DIM), lambda bi, h, i: (0, 0)),
            pl.BlockSpec((1, 2 * HEAD_DIM), lambda bi, h, i: (0, 0)),
            pl.BlockSpec((tq, LANES), lambda bi, h, i: (bi * nq + i, qb + h)),
            pl.BlockSpec((l, LANES), lambda bi, h, i: (bi, kb + h)),
            pl.BlockSpec((l, LANES), lambda bi, h, i: (bi, vb + h)),
            pl.BlockSpec((tq, LANES), lambda bi, h, i: (bi * nq + i, zb + h)),
        ],
        out_specs=pl.BlockSpec((tq, LANES), lambda bi, h, i: (bi * nq + i, h)),
        scratch_shapes=[pltpu.VMEM((2, l // tk, tq, tk), F32)],
        compiler_params=_params("parallel", "parallel", "arbitrary"),
        name="diffattn",
    )(lam_qk, g_diff, u, u, u, u)


def _dil_slopes():
    n = DIL_GROUPS * DIL_HEADS
    return [float(np.float32(2.0 ** (-8.0 * (i + 1) / n))) for i in range(n)]


def _dilated_kernel(q_ref, k_ref, v_ref, o_ref, lse_ref, *, m, tq, tw, slopes):
    j = pl.program_id(2)
    q0pos = j * tq
    kstart = pl.multiple_of(jnp.clip(q0pos - DIL_SIDE, 0, m - tw), DIL_SIDE)
    rel = (kstart + lax.broadcasted_iota(jnp.int32, (tq, tw), 1)
           - q0pos - lax.broadcasted_iota(jnp.int32, (tq, tw), 0))
    dist = jnp.abs(rel).astype(F32)
    ok = dist <= float(DIL_SIDE)
    lo, hi = _head_masks(HEAD_DIM ** -0.5)
    lane = lax.broadcasted_iota(jnp.int32, (tq, LANES), 1)
    for p in range(DIL_HEADS // 2):
        cols = slice(p * LANES, (p + 1) * LANES)
        q = q_ref[:, cols]
        k = k_ref[pl.ds(kstart, tw), cols]
        v = v_ref[pl.ds(kstart, tw), cols]
        outs, lses = [], []
        for half, mask in enumerate((lo, hi)):
            s = lax.dot_general(q * mask, k, _NT, preferred_element_type=F32)
            s = jnp.where(ok, s - slopes[2 * p + half] * dist, NEG_INF)
            mx = jnp.max(s, axis=-1, keepdims=True)
            e = jnp.exp(s - mx)
            l = jnp.sum(e, axis=-1, keepdims=True)
            outs.append(jnp.dot(e.astype(BF16), v, preferred_element_type=F32) * (1.0 / l))
            lses.append(mx + jnp.log(l))
        o_ref[:, cols] = jnp.where(lane < HEAD_DIM, outs[0], outs[1])
        lse_ref[:, cols] = jnp.where(lane < HEAD_DIM, lses[0], lses[1])


def _dilated_group(u, b, l, g):
    dil = DIL_PAIRS[g][1]
    assert DIL_PAIRS[g][0] // (2 * dil) == DIL_SIDE
    m = l // dil
    tq = min(DIL_TQ, m)
    tw = min(tq + 2 * DIL_SIDE, m)
    nq = m // tq
    wblk = IN_WIDTH // C_W
    qc, kc, vc = (off // C_W + g for off in (OFF_QC, OFF_KC, OFF_VC))
    uv = u.reshape(b * m, dil * IN_WIDTH)
    slopes = tuple(s * dil for s in _dil_slopes()[g * DIL_HEADS:(g + 1) * DIL_HEADS])
    o, lse = pl.pallas_call(
        functools.partial(_dilated_kernel, m=m, tq=tq, tw=tw, slopes=slopes),
        out_shape=[jax.ShapeDtypeStruct((b * m, dil * C_W), F32)] * 2,
        grid=(b, dil, nq),
        in_specs=[
            pl.BlockSpec((tq, C_W), lambda bi, r, j: (bi * nq + j, r * wblk + qc)),
            pl.BlockSpec((m, C_W), lambda bi, r, j: (bi, r * wblk + kc)),
            pl.BlockSpec((m, C_W), lambda bi, r, j: (bi, r * wblk + vc)),
        ],
        out_specs=[pl.BlockSpec((tq, C_W), lambda bi, r, j: (bi * nq + j, r))] * 2,
        compiler_params=_params("parallel", "parallel", "arbitrary"),
        name=f"dilated{g}",
    )(uv, uv, uv)
    return o.reshape(b * l, C_W), lse.reshape(b * l, C_W)


def _merge_kernel(x_ref, ga_ref, gb_ref, o0_ref, o1_ref, o2_ref, l0_ref, l1_ref, l2_ref, zc_ref, gl_ref,
                  bg_ref, wa_ref, wb_ref, wc_ref, wo_ref, gf_ref, out_ref, *, final_norm):
    lses = (l0_ref[...], l1_ref[...], l2_ref[...])
    mx = jnp.maximum(jnp.maximum(lses[0], lses[1]), lses[2])
    wts = [jnp.exp(x - mx) for x in lses]
    den = wts[0] + wts[1] + wts[2]
    yc = (wts[0] * o0_ref[...] + wts[1] * o1_ref[...] + wts[2] * o2_ref[...]) / den
    gc = (yc * _silu(zc_ref[...].astype(F32))).astype(BF16)
    branches = (jnp.dot(ga_ref[...], wa_ref[...], preferred_element_type=F32),
                jnp.dot(gb_ref[...], wb_ref[...], preferred_element_type=F32),
                jnp.dot(gc, wc_ref[...], preferred_element_type=F32))
    merged = None
    for n, pn in enumerate(branches):
        logit = gl_ref[:, n * D_MODEL:(n + 1) * D_MODEL].astype(F32) + bg_ref[n:n + 1, :]
        term = pn / (1.0 + jnp.exp(-logit))
        merged = term if merged is None else merged + term
    y = x_ref[...] + jnp.dot(merged.astype(BF16), wo_ref[...], preferred_element_type=F32)
    if final_norm:
        ms = jnp.mean(y * y, axis=-1, keepdims=True)
        y = y * lax.rsqrt(ms + RMS_EPS) * gf_ref[...]
    out_ref[...] = y


def _merge(x, ga, gb, dil, u, b_gate, wa, wb, wc, wo, g_final, final_norm):
    t = x.shape[0]
    tm = min(MERGE_TM, t)
    row = lambda w, col=0: pl.BlockSpec((tm, w), lambda i: (i, col))
    full = lambda a: pl.BlockSpec(a.shape, lambda i: (0, 0))
    (o0, l0), (o1, l1), (o2, l2) = dil
    return pl.pallas_call(
        functools.partial(_merge_kernel, final_norm=final_norm),
        out_shape=jax.ShapeDtypeStruct((t, D_MODEL), F32),
        grid=(t // tm,),
        in_specs=[row(D_MODEL), row(A_W), row(B_W)] + [row(C_W)] * 6
                 + [row(C_W, OFF_ZC // C_W), row(GL_W, OFF_GL // GL_W)]
                 + [full(a) for a in (b_gate, wa, wb, wc, wo, g_final)],
        out_specs=row(D_MODEL),
        compiler_params=_params("parallel"),
        name="merge",
    )(x, ga, gb, o0, o1, o2, l0, l1, l2, u, u, b_gate, wa, wb, wc, wo, g_final)


def _trunk(x, layers, g_final):
    b, l, _ = x.shape
    x = x.reshape(b * l, D_MODEL)
    for n, p in enumerate(layers):
        lam_init = 0.8 - 0.6 * math.exp(-0.3 * n)
        u = _inproj(x, p["g_norm"], p["w_in"])
        ga = _natten(u, p["na_bias"], b, l)
        gb = _diff(u, p["lam_qk"], p["g_diff"], b, l, lam_init)
        dil = [_dilated_group(u, b, l, g) for g in range(DIL_GROUPS)]
        x = _merge(x, ga, gb, dil, u, p["b_gate"], p["w_br_a"], p["w_br_b"], p["w_br_c"], p["w_out"],
                   g_final, final_norm=(n == len(layers) - 1))
    return x.reshape(b, l, D_MODEL)


def _prepare_layers(g_norm, w_in, b_gate, rpb, lam_qk, g_diff, w_br_a, w_br_b, w_br_c, w_out):
    gl0 = IN_WIDTH - GL_W
    layers = []
    for n in range(w_in.shape[0]):
        layers.append(dict(
            g_norm=g_norm[n].reshape(1, D_MODEL).astype(F32),
            w_in=jnp.concatenate([w_in[n][:, gl0:], w_in[n][:, :gl0]], axis=1).astype(BF16),
            b_gate=b_gate[n].astype(F32),
            na_bias=_natten_bias(rpb[n]),
            lam_qk=lam_qk[n].astype(F32),
            g_diff=g_diff[n].reshape(1, 2 * HEAD_DIM).astype(F32),
            w_br_a=w_br_a[n].astype(BF16),
            w_br_b=w_br_b[n].astype(BF16),
            w_br_c=w_br_c[n].astype(BF16),
            w_out=w_out[n].astype(BF16),
        ))
    return layers


def kernel(x_prompt, x_sample, g_norm, w_in, b_gate, rpb, lam_qk, g_diff, w_br_a, w_br_b, w_br_c, w_out, g_final):
    layers = _prepare_layers(g_norm, w_in, b_gate, rpb, lam_qk, g_diff, w_br_a, w_br_b, w_br_c, w_out)
    gf = g_final.reshape(1, D_MODEL).astype(F32)
    return (_trunk(x_prompt, layers, gf), _trunk(x_sample, layers, gf))
```

```python
import functools
import math

import numpy as np
import jax
import jax.numpy as jnp
from jax import lax
from jax.experimental import pallas as pl
from jax.experimental.pallas import tpu as pltpu

F32 = jnp.float32
BF16 = jnp.bfloat16

D_MODEL = 1024
GRID_W = 64
HEAD_DIM = 64
RMS_EPS = 1e-6
NEG_INF = -1e30
LANES = 128

NA_HEADS = 8
NA_WIN_ROWS = 8
NA_WIN_COLS = 16
NA_Q_ROWS = 4
NA_K_ROWS = NA_Q_ROWS + NA_WIN_ROWS
NA_TQ = NA_Q_ROWS * GRID_W
NA_TK = NA_K_ROWS * GRID_W

DIFF_HEADS = 4
DIFF_TQ = 256
DIFF_TK = 512

DIL_PAIRS = ((128, 1), (512, 4), (2048, 16))
DIL_GROUPS = 3
DIL_HEADS = 4
DIL_SIDE = 64
DIL_TQ = 256

A_W = NA_HEADS * HEAD_DIM
B_W = DIFF_HEADS * 2 * HEAD_DIM
C_QKV = DIL_GROUPS * DIL_HEADS * HEAD_DIM
C_W = DIL_HEADS * HEAD_DIM
N_BRANCH = 3
GL_W = N_BRANCH * D_MODEL
IN_WIDTH = 4 * A_W + 4 * B_W + 3 * C_QKV + C_W + GL_W

OFF_GL = 0
OFF_QA = GL_W
OFF_KA = OFF_QA + A_W
OFF_VA = OFF_KA + A_W
OFF_ZA = OFF_VA + A_W
OFF_QB = OFF_ZA + A_W
OFF_KB = OFF_QB + B_W
OFF_VB = OFF_KB + B_W
OFF_ZB = OFF_VB + B_W
OFF_ZC = OFF_ZB + B_W
OFF_C0 = OFF_ZC + C_W
MAIN_W = OFF_C0 + 3 * C_W
DIL_W = 3 * C_W
assert MAIN_W + 2 * DIL_W == IN_WIDTH

VMEM_LIMIT = 56 * 1024 * 1024

IN_TM = 1024
IN_TN = MAIN_W // 4
MERGE_TM = 512

_NT = (((1,), (1,)), ((), ()))


def _column_segments():
    b0 = 4 * A_W
    c0 = b0 + 4 * B_W
    zc0 = c0 + 3 * C_QKV
    qkv = lambda g: [(c0 + n * C_QKV + g * C_W, C_W) for n in range(3)]
    return [(zc0 + C_W, GL_W), (0, 4 * A_W), (b0, 4 * B_W), (zc0, C_W)] + qkv(0) + qkv(1) + qkv(2)


def _params(*sem):
    return pltpu.CompilerParams(dimension_semantics=sem, vmem_limit_bytes=VMEM_LIMIT)


def _silu(z):
    return z / (1.0 + jnp.exp(-z))


def _rms_scale(x):
    return x * lax.rsqrt(jnp.mean(x * x, axis=-1, keepdims=True) + RMS_EPS)


def _head_masks(scale):
    lane = lax.broadcasted_iota(jnp.int32, (1, LANES), 1)
    lo = jnp.where(lane < HEAD_DIM, scale, 0.0).astype(BF16)
    hi = jnp.where(lane >= HEAD_DIM, scale, 0.0).astype(BF16)
    return lo, hi


def _lane_tiles(x):
    return [x[:, n * LANES:(n + 1) * LANES] for n in range(x.shape[-1] // LANES)]


def _inproj_kernel(x_ref, g_ref, w_ref, o_ref, h_ref):
    @pl.when(pl.program_id(1) == 0)
    def _():
        h_ref[...] = (_rms_scale(x_ref[...]) * g_ref[...]).astype(BF16)

    o_ref[...] = jnp.dot(h_ref[...], w_ref[...], preferred_element_type=F32).astype(BF16)


def _inproj(x, g, w):
    t = x.shape[0]
    tm = min(IN_TM, t)
    return pl.pallas_call(
        _inproj_kernel,
        out_shape=jax.ShapeDtypeStruct((t, MAIN_W), BF16),
        grid=(t // tm, MAIN_W // IN_TN),
        in_specs=[
            pl.BlockSpec((tm, D_MODEL), lambda i, j: (i, 0)),
            pl.BlockSpec((1, D_MODEL), lambda i, j: (0, 0)),
            pl.BlockSpec((D_MODEL, IN_TN), lambda i, j: (0, j)),
        ],
        out_specs=pl.BlockSpec((tm, IN_TN), lambda i, j: (i, j)),
        scratch_shapes=[pltpu.VMEM((tm, D_MODEL), BF16)],
        compiler_params=_params("parallel", "arbitrary"),
        name="inproj",
    )(x, g, w)


def _inproj_dil_kernel(x_ref, g_ref, w_ref, o1_ref, o2_ref, r_ref, *, tm):
    h = (_rms_scale(x_ref[...]) * g_ref[...]).astype(BF16)
    res = jnp.dot(h, w_ref[...], preferred_element_type=F32)
    for t, tile in enumerate(_lane_tiles(res)):
        r_ref[t] = tile
    tiles_per_group = DIL_W // LANES
    for g, o_ref in ((1, o1_ref), (2, o2_ref)):
        dil = DIL_PAIRS[g][1]
        for t in range(tiles_per_group):
            for r in range(dil):
                rows = r_ref[(g - 1) * tiles_per_group + t, pl.ds(r, tm // dil, stride=dil), :]
                o_ref[r, :, t * LANES:(t + 1) * LANES] = rows.astype(BF16)


def _inproj_dil(x, g, w, b, l):
    tm = min(IN_TM, l)
    nt = l // tm
    d1, d2 = DIL_PAIRS[1][1], DIL_PAIRS[2][1]
    return pl.pallas_call(
        functools.partial(_inproj_dil_kernel, tm=tm),
        out_shape=[jax.ShapeDtypeStruct((b, d1, l // d1, DIL_W), BF16),
                   jax.ShapeDtypeStruct((b, d2, l // d2, DIL_W), BF16)],
        grid=(b, nt),
        in_specs=[
            pl.BlockSpec((tm, D_MODEL), lambda bi, i: (bi * nt + i, 0)),
            pl.BlockSpec((1, D_MODEL), lambda bi, i: (0, 0)),
            pl.BlockSpec((D_MODEL, 2 * DIL_W), lambda bi, i: (0, 0)),
        ],
        out_specs=[pl.BlockSpec((None, d1, tm // d1, DIL_W), lambda bi, i: (bi, 0, i, 0)),
                   pl.BlockSpec((None, d2, tm // d2, DIL_W), lambda bi, i: (bi, 0, i, 0))],
        scratch_shapes=[pltpu.VMEM((2 * DIL_W // LANES, tm, LANES), F32)],
        compiler_params=_params("parallel", "parallel"),
        name="inproj_dil",
    )(x, g, w)


def _natten_bias(rpb):
    ri = np.arange(NA_Q_ROWS)[:, None]
    kr = np.arange(NA_K_ROWS)[None, :]
    c = np.arange(GRID_W)[:, None]
    kc = np.arange(GRID_W)[None, :]
    qstart = np.clip(c - NA_WIN_COLS // 2, 0, GRID_W - NA_WIN_COLS)
    col_ok = (kc >= qstart) & (kc < qstart + NA_WIN_COLS)
    col_idx = np.clip(kc - c + NA_WIN_COLS - 1, 0, 2 * NA_WIN_COLS - 2)
    col_sel = (col_idx[:, :, None] == np.arange(2 * NA_WIN_COLS - 1)) & col_ok[:, :, None]
    row_sel, row_ok = [], []
    for variant, q_minus_k_row0 in enumerate((0, NA_WIN_ROWS // 2, NA_WIN_ROWS)):
        first = (0 * ri, ri, 0 * ri + NA_K_ROWS - NA_WIN_ROWS)[variant]
        ok = (kr >= first) & (kr < first + NA_WIN_ROWS)
        idx = np.clip(kr - q_minus_k_row0 - ri + NA_WIN_ROWS - 1, 0, 2 * NA_WIN_ROWS - 2)
        row_sel.append((idx[:, :, None] == np.arange(2 * NA_WIN_ROWS - 1)) & ok[:, :, None])
        row_ok.append(ok)
    row_sel = jnp.asarray(np.stack(row_sel), F32)
    vals = jnp.einsum("vrka,ham,cjm->vhrckj", row_sel, rpb.astype(F32), jnp.asarray(col_sel, F32),
                      precision=lax.Precision.HIGHEST)
    ok = np.stack(row_ok)[:, None, :, None, :, None] & col_ok[None, None, None, :, None, :]
    vals = jnp.where(jnp.asarray(ok), vals, NEG_INF)
    return vals.reshape(3, NA_HEADS, NA_TQ, NA_TK)


def _natten_kernel(bias_ref, q_ref, k_ref, v_ref, z_ref, o_ref, *, rows):
    i = pl.program_id(1)
    krow0 = jnp.clip(i * NA_Q_ROWS - NA_WIN_ROWS // 2, 0, rows - NA_K_ROWS)
    kstart = pl.multiple_of(krow0 * GRID_W, GRID_W)
    lo, hi = _head_masks(HEAD_DIM ** -0.5)
    lane = lax.broadcasted_iota(jnp.int32, (NA_TQ, LANES), 1)
    for p in range(NA_HEADS // 2):
        cols = slice(p * LANES, (p + 1) * LANES)
        q = q_ref[:, cols]
        k = k_ref[pl.ds(kstart, NA_TK), cols]
        v = v_ref[pl.ds(kstart, NA_TK), cols]
        halves = []
        for half, mask in enumerate((lo, hi)):
            s = lax.dot_general(q * mask, k, _NT, preferred_element_type=F32)
            s = s + bias_ref[2 * p + half]
            m = jnp.max(s, axis=-1, keepdims=True)
            e = jnp.exp(s - m)
            l = jnp.sum(e, axis=-1, keepdims=True)
            o = jnp.dot(e.astype(BF16), v, preferred_element_type=F32)
            halves.append(o * (1.0 / l))
        y = jnp.where(lane < HEAD_DIM, halves[0], halves[1])
        o_ref[:, cols] = (y * _silu(z_ref[:, cols].astype(F32))).astype(BF16)


def _natten(u, bias, b, l):
    rows = l // GRID_W
    nblk = l // NA_TQ
    assert rows >= NA_K_ROWS and nblk >= 3

    def bias_map(bi, i):
        return (jnp.where(i == 0, 0, jnp.where(i == nblk - 1, 2, 1)), 0, 0, 0)

    return pl.pallas_call(
        functools.partial(_natten_kernel, rows=rows),
        out_shape=jax.ShapeDtypeStruct((b * l, A_W), BF16),
        grid=(b, nblk),
        in_specs=[
            pl.BlockSpec((None, NA_HEADS, NA_TQ, NA_TK), bias_map),
            pl.BlockSpec((NA_TQ, A_W), lambda bi, i: (bi * nblk + i, OFF_QA // A_W)),
            pl.BlockSpec((l, A_W), lambda bi, i: (bi, OFF_KA // A_W)),
            pl.BlockSpec((l, A_W), lambda bi, i: (bi, OFF_VA // A_W)),
            pl.BlockSpec((NA_TQ, A_W), lambda bi, i: (bi * nblk + i, OFF_ZA // A_W)),
        ],
        out_specs=pl.BlockSpec((NA_TQ, A_W), lambda bi, i: (bi * nblk + i, 0)),
        compiler_params=_params("parallel", "arbitrary"),
        name="natten",
    )(bias, u, u, u, u)


def _diff_slopes():
    return [float(np.float32(2.0 ** (-8.0 * (i + 1) / DIFF_HEADS))) for i in range(DIFF_HEADS)]


def _diff_bias_table(l, tq, tk):
    nvar = (l - tk) // tq + (l - tq) // tq + 1
    n0 = (l - tk) // tq
    delta = (jnp.arange(nvar, dtype=jnp.int32) - n0) * tq
    q_minus_k = jnp.arange(tq, dtype=jnp.int32)[:, None] - jnp.arange(tk, dtype=jnp.int32)[None, :]
    dist = jnp.abs(q_minus_k[None] + delta[:, None, None]).astype(F32)
    slopes = jnp.asarray(_diff_slopes(), F32)
    return -slopes[:, None, None, None] * dist[None], n0


def _diff_kernel(lq_ref, gd_ref, bias_ref, q_ref, k_ref, v_ref, z_ref, o_ref, s_ref, *,
                 l, tq, tk, n0, lam_init):
    i = pl.program_id(2)
    nck = l // tk
    lq = lq_ref[...]
    lam = (jnp.exp(jnp.sum(lq[0:1] * lq[1:2], axis=-1, keepdims=True))
           - jnp.exp(jnp.sum(lq[2:3] * lq[3:4], axis=-1, keepdims=True)) + lam_init)
    lo, hi = _head_masks(HEAD_DIM ** -0.5)
    q = q_ref[...]
    qs = (q * lo, q * hi)

    mx = [None, None]
    for c in range(nck):
        k = k_ref[c * tk:(c + 1) * tk, :]
        bias = bias_ref[i + (n0 - c * (tk // tq))]
        for n in range(2):
            s = lax.dot_general(qs[n], k, _NT, preferred_element_type=F32) + bias
            s_ref[n, c] = s
            for t in _lane_tiles(s):
                mx[n] = t if mx[n] is None else jnp.maximum(mx[n], t)
    m = [jnp.max(x, axis=-1, keepdims=True) for x in mx]

    sm = [None, None]
    for c in range(nck):
        for n in range(2):
            e = jnp.exp(s_ref[n, c] - m[n])
            s_ref[n, c] = e
            for t in _lane_tiles(e):
                sm[n] = t if sm[n] is None else sm[n] + t
    a0 = 1.0 / jnp.sum(sm[0], axis=-1, keepdims=True)
    a1 = lam / jnp.sum(sm[1], axis=-1, keepdims=True)

    o = None
    for c in range(nck):
        w = (s_ref[0, c] * a0 - s_ref[1, c] * a1).astype(BF16)
        pv = jnp.dot(w, v_ref[c * tk:(c + 1) * tk, :], preferred_element_type=F32)
        o = pv if o is None else o + pv

    y = (_rms_scale(o) * gd_ref[...]) * (1.0 - lam_init)
    o_ref[...] = (y * _silu(z_ref[...].astype(F32))).astype(BF16)


def _diff(u, lam_qk, g_diff, b, l, lam_init):
    tq = min(DIFF_TQ, l)
    tk = min(DIFF_TK, l)
    nq = l // tq
    qb, kb, vb, zb = (off // LANES for off in (OFF_QB, OFF_KB, OFF_VB, OFF_ZB))
    bias, n0 = _diff_bias_table(l, tq, tk)
    nvar = bias.shape[1]
    return pl.pallas_call(
        functools.partial(_diff_kernel, l=l, tq=tq, tk=tk, n0=n0, lam_init=lam_init),
        out_shape=jax.ShapeDtypeStruct((b * l, B_W), BF16),
        grid=(DIFF_HEADS, b, nq),
        in_specs=[
            pl.BlockSpec((4, HEAD_DIM), lambda h, bi, i: (0, 0)),
            pl.BlockSpec((1, 2 * HEAD_DIM), lambda h, bi, i: (0, 0)),
            pl.BlockSpec((None, nvar, tq, tk), lambda h, bi, i: (h, 0, 0, 0), pipeline_mode=pl.Buffered(1)),
            pl.BlockSpec((tq, LANES), lambda h, bi, i: (bi * nq + i, qb + h)),
            pl.BlockSpec((l, LANES), lambda h, bi, i: (bi, kb + h)),
            pl.BlockSpec((l, LANES), lambda h, bi, i: (bi, vb + h)),
            pl.BlockSpec((tq, LANES), lambda h, bi, i: (bi * nq + i, zb + h)),
        ],
        out_specs=pl.BlockSpec((tq, LANES), lambda h, bi, i: (bi * nq + i, h)),
        scratch_shapes=[pltpu.VMEM((2, l // tk, tq, tk), F32)],
        compiler_params=_params("parallel", "parallel", "arbitrary"),
        name="diffattn",
    )(lam_qk, g_diff, bias, u, u, u, u)


def _dil_slopes():
    n = DIL_GROUPS * DIL_HEADS
    return [float(np.float32(2.0 ** (-8.0 * (i + 1) / n))) for i in range(n)]


def _dilated_kernel(q_ref, k_ref, v_ref, o_ref, lse_ref, *, m, tq, tw, slopes):
    j = pl.program_id(2)
    q0pos = j * tq
    kstart = pl.multiple_of(jnp.clip(q0pos - DIL_SIDE, 0, m - tw), DIL_SIDE)
    rel = (kstart + lax.broadcasted_iota(jnp.int32, (tq, tw), 1)
           - q0pos - lax.broadcasted_iota(jnp.int32, (tq, tw), 0))
    dist = jnp.abs(rel).astype(F32)
    ok = dist <= float(DIL_SIDE)
    lo, hi = _head_masks(HEAD_DIM ** -0.5)
    lane = lax.broadcasted_iota(jnp.int32, (tq, LANES), 1)
    for p in range(DIL_HEADS // 2):
        cols = slice(p * LANES, (p + 1) * LANES)
        q = q_ref[:, cols]
        k = k_ref[pl.ds(kstart, tw), cols]
        v = v_ref[pl.ds(kstart, tw), cols]
        outs, lses = [], []
        for half, mask in enumerate((lo, hi)):
            s = lax.dot_general(q * mask, k, _NT, preferred_element_type=F32)
            s = jnp.where(ok, s - slopes[2 * p + half] * dist, NEG_INF)
            mx = jnp.max(s, axis=-1, keepdims=True)
            e = jnp.exp(s - mx)
            l = jnp.sum(e, axis=-1, keepdims=True)
            outs.append(jnp.dot(e.astype(BF16), v, preferred_element_type=F32) * (1.0 / l))
            lses.append(mx + jnp.log(l))
        o_ref[:, cols] = jnp.where(lane < HEAD_DIM, outs[0], outs[1])
        lse_ref[:, cols] = jnp.where(lane < HEAD_DIM, lses[0], lses[1])


def _dilated_group(qkv, col0, b, l, g):
    dil = DIL_PAIRS[g][1]
    assert DIL_PAIRS[g][0] // (2 * dil) == DIL_SIDE
    m = l // dil
    tq = min(DIL_TQ, m)
    tw = min(tq + 2 * DIL_SIDE, m)
    slopes = tuple(s * dil for s in _dil_slopes()[g * DIL_HEADS:(g + 1) * DIL_HEADS])
    return pl.pallas_call(
        functools.partial(_dilated_kernel, m=m, tq=tq, tw=tw, slopes=slopes),
        out_shape=[jax.ShapeDtypeStruct((b, dil, m, C_W), F32)] * 2,
        grid=(b, dil, m // tq),
        in_specs=[
            pl.BlockSpec((None, None, tq, C_W), lambda bi, r, j: (bi, r, j, col0)),
            pl.BlockSpec((None, None, m, C_W), lambda bi, r, j: (bi, r, 0, col0 + 1)),
            pl.BlockSpec((None, None, m, C_W), lambda bi, r, j: (bi, r, 0, col0 + 2)),
        ],
        out_specs=[pl.BlockSpec((None, None, tq, C_W), lambda bi, r, j: (bi, r, j, 0))] * 2,
        compiler_params=_params("parallel", "parallel", "arbitrary"),
        name=f"dilated{g}",
    )(qkv, qkv, qkv)


def _merge_kernel(x_ref, ga_ref, gb_ref, o0_ref, l0_ref, o1_ref, l1_ref, o2_ref, l2_ref, zc_ref, gl_ref,
                  bg_ref, wa_ref, wb_ref, wc_ref, wo_ref, gf_ref, out_ref, *scratch, tm, final_norm):
    outs, lses = [o0_ref[0]], [l0_ref[0]]
    for g, refs in ((1, (o1_ref, l1_ref)), (2, (o2_ref, l2_ref))):
        dil = DIL_PAIRS[g][1]
        for src, dst in zip(refs, scratch[2 * (g - 1):2 * g]):
            for t in range(C_W // LANES):
                for r in range(dil):
                    dst[t, pl.ds(r, tm // dil, stride=dil), :] = src[r, :, t * LANES:(t + 1) * LANES]
        outs.append(jnp.concatenate([scratch[2 * (g - 1)][t] for t in range(C_W // LANES)], axis=-1))
        lses.append(jnp.concatenate([scratch[2 * (g - 1) + 1][t] for t in range(C_W // LANES)], axis=-1))
    mx = jnp.maximum(jnp.maximum(lses[0], lses[1]), lses[2])
    wts = [jnp.exp(x - mx) for x in lses]
    den = wts[0] + wts[1] + wts[2]
    yc = (wts[0] * outs[0] + wts[1] * outs[1] + wts[2] * outs[2]) / den
    gc = (yc * _silu(zc_ref[...].astype(F32))).astype(BF16)
    branches = (jnp.dot(ga_ref[...], wa_ref[...], preferred_element_type=F32),
                jnp.dot(gb_ref[...], wb_ref[...], preferred_element_type=F32),
                jnp.dot(gc, wc_ref[...], preferred_element_type=F32))
    merged = None
    for n, pn in enumerate(branches):
        logit = gl_ref[:, n * D_MODEL:(n + 1) * D_MODEL].astype(F32) + bg_ref[n:n + 1, :]
        term = pn / (1.0 + jnp.exp(-logit))
        merged = term if merged is None else merged + term
    y = x_ref[...] + jnp.dot(merged.astype(BF16), wo_ref[...], preferred_element_type=F32)
    if final_norm:
        y = _rms_scale(y) * gf_ref[...]
    out_ref[...] = y


def _merge(x, ga, gb, dil, u, b_gate, wa, wb, wc, wo, g_final, b, l, final_norm):
    tm = min(MERGE_TM, l)
    nt = l // tm
    row = lambda w, col=0: pl.BlockSpec((tm, w), lambda bi, i: (bi * nt + i, col))
    full = lambda a: pl.BlockSpec(a.shape, lambda bi, i: (0, 0))
    cm = lambda g: pl.BlockSpec((None, DIL_PAIRS[g][1], tm // DIL_PAIRS[g][1], C_W), lambda bi, i: (bi, 0, i, 0))
    dil_specs = [cm(g) for g in range(DIL_GROUPS) for _ in range(2)]
    dil_args = [a for pair in dil for a in pair]
    return pl.pallas_call(
        functools.partial(_merge_kernel, tm=tm, final_norm=final_norm),
        out_shape=jax.ShapeDtypeStruct((b * l, D_MODEL), F32),
        grid=(b, nt),
        in_specs=[row(D_MODEL), row(A_W), row(B_W)] + dil_specs
                 + [row(C_W, OFF_ZC // C_W), row(GL_W, OFF_GL // GL_W)]
                 + [full(a) for a in (b_gate, wa, wb, wc, wo, g_final)],
        out_specs=row(D_MODEL),
        scratch_shapes=[pltpu.VMEM((C_W // LANES, tm, LANES), F32)] * 4,
        compiler_params=_params("parallel", "parallel"),
        name="merge",
    )(x, ga, gb, *dil_args, u, u, b_gate, wa, wb, wc, wo, g_final)


def _trunk(x, layers, g_final):
    b, l, _ = x.shape
    x = x.reshape(b * l, D_MODEL)
    for n, p in enumerate(layers):
        lam_init = 0.8 - 0.6 * math.exp(-0.3 * n)
        u = _inproj(x, p["g_norm"], p["w_main"])
        c1, c2 = _inproj_dil(x, p["g_norm"], p["w_dil"], b, l)
        ga = _natten(u, p["na_bias"], b, l)
        gb = _diff(u, p["lam_qk"], p["g_diff"], b, l, lam_init)
        dil = [_dilated_group(u.reshape(b, 1, l, MAIN_W), OFF_C0 // C_W, b, l, 0),
               _dilated_group(c1, 0, b, l, 1),
               _dilated_group(c2, 0, b, l, 2)]
        x = _merge(x, ga, gb, dil, u, p["b_gate"], p["w_br_a"], p["w_br_b"], p["w_br_c"], p["w_out"],
                   g_final, b, l, final_norm=(n == len(layers) - 1))
    return x.reshape(b, l, D_MODEL)


def _prepare_layers(g_norm, w_in, b_gate, rpb, lam_qk, g_diff, w_br_a, w_br_b, w_br_c, w_out):
    segments = _column_segments()
    layers = []
    for n in range(w_in.shape[0]):
        w = jnp.concatenate([w_in[n][:, s:s + width] for s, width in segments], axis=1).astype(BF16)
        layers.append(dict(
            g_norm=g_norm[n].reshape(1, D_MODEL).astype(F32),
            w_main=w[:, :MAIN_W],
            w_dil=w[:, MAIN_W:],
            b_gate=b_gate[n].astype(F32),
            na_bias=_natten_bias(rpb[n]),
            lam_qk=lam_qk[n].astype(F32),
            g_diff=g_diff[n].reshape(1, 2 * HEAD_DIM).astype(F32),
            w_br_a=w_br_a[n].astype(BF16),
            w_br_b=w_br_b[n].astype(BF16),
            w_br_c=w_br_c[n].astype(BF16),
            w_out=w_out[n].astype(BF16),
        ))
    return layers


def kernel(x_prompt, x_sample, g_norm, w_in, b_gate, rpb, lam_qk, g_diff, w_br_a, w_br_b, w_br_c, w_out, g_final):
    layers = _prepare_layers(g_norm, w_in, b_gate, rpb, lam_qk, g_diff, w_br_a, w_br_b, w_br_c, w_out)
    gf = g_final.reshape(1, D_MODEL).astype(F32)
    return (_trunk(x_prompt, layers, gf), _trunk(x_sample, layers, gf))
```

```python
import functools
import math

import numpy as np
import jax
import jax.numpy as jnp
from jax import lax
from jax.experimental import pallas as pl
from jax.experimental.pallas import tpu as pltpu

F32 = jnp.float32
BF16 = jnp.bfloat16

D_MODEL = 1024
GRID_W = 64
HEAD_DIM = 64
RMS_EPS = 1e-6
NEG_INF = -1e30
LANES = 128

NA_HEADS = 8
NA_WIN_ROWS = 8
NA_WIN_COLS = 16
NA_Q_ROWS = 4
NA_K_ROWS = NA_Q_ROWS + NA_WIN_ROWS
NA_TQ = NA_Q_ROWS * GRID_W
NA_TK = NA_K_ROWS * GRID_W

DIFF_HEADS = 4
DIFF_TQ = 256
DIFF_TK = 512

DIL_PAIRS = ((128, 1), (512, 4), (2048, 16))
DIL_GROUPS = 3
DIL_HEADS = 4
DIL_SIDE = 64
DIL_TQ = 256

A_W = NA_HEADS * HEAD_DIM
B_W = DIFF_HEADS * 2 * HEAD_DIM
C_QKV = DIL_GROUPS * DIL_HEADS * HEAD_DIM
C_W = DIL_HEADS * HEAD_DIM
N_BRANCH = 3
GL_W = N_BRANCH * D_MODEL
IN_WIDTH = 4 * A_W + 4 * B_W + 3 * C_QKV + C_W + GL_W

OFF_GL = 0
OFF_QA = GL_W
OFF_KA = OFF_QA + A_W
OFF_VA = OFF_KA + A_W
OFF_ZA = OFF_VA + A_W
OFF_QB = OFF_ZA + A_W
OFF_KB = OFF_QB + B_W
OFF_VB = OFF_KB + B_W
OFF_ZB = OFF_VB + B_W
OFF_ZC = OFF_ZB + B_W
OFF_C0 = OFF_ZC + C_W
MAIN_W = OFF_C0 + 3 * C_W
DIL_W = 3 * C_W
assert MAIN_W + 2 * DIL_W == IN_WIDTH

VMEM_LIMIT = 56 * 1024 * 1024

IN_TM = 1024
IN_TN = MAIN_W // 4
MERGE_TM = 512

_NT = (((1,), (1,)), ((), ()))
LOG2E = math.log2(math.e)
Q_SCALE = HEAD_DIM ** -0.5 * LOG2E


def _column_segments():
    b0 = 4 * A_W
    c0 = b0 + 4 * B_W
    zc0 = c0 + 3 * C_QKV
    qkv = lambda g: [(c0 + n * C_QKV + g * C_W, C_W, Q_SCALE if n == 0 else 1.0) for n in range(3)]
    return ([(zc0 + C_W, GL_W, 1.0), (0, A_W, Q_SCALE), (A_W, 3 * A_W, 1.0), (b0, B_W, Q_SCALE),
             (b0 + B_W, 3 * B_W, 1.0), (zc0, C_W, 1.0)] + qkv(0) + qkv(1) + qkv(2))


def _params(*sem):
    return pltpu.CompilerParams(dimension_semantics=sem, vmem_limit_bytes=VMEM_LIMIT)


def _silu(z):
    return z / (1.0 + jnp.exp(-z))


def _rms_scale(x):
    return x * lax.rsqrt(jnp.mean(x * x, axis=-1, keepdims=True) + RMS_EPS)


def _head_masks(scale):
    lane = lax.broadcasted_iota(jnp.int32, (1, LANES), 1)
    lo = jnp.where(lane < HEAD_DIM, scale, 0.0).astype(BF16)
    hi = jnp.where(lane >= HEAD_DIM, scale, 0.0).astype(BF16)
    return lo, hi


def _lane_tiles(x):
    return [x[:, n * LANES:(n + 1) * LANES] for n in range(x.shape[-1] // LANES)]


def _inproj_kernel(x_ref, g_ref, w_ref, o_ref, h_ref):
    @pl.when(pl.program_id(1) == 0)
    def _():
        h_ref[...] = (_rms_scale(x_ref[...]) * g_ref[...]).astype(BF16)

    o_ref[...] = jnp.dot(h_ref[...], w_ref[...], preferred_element_type=F32).astype(BF16)


def _inproj(x, g, w):
    t = x.shape[0]
    tm = min(IN_TM, t)
    return pl.pallas_call(
        _inproj_kernel,
        out_shape=jax.ShapeDtypeStruct((t, MAIN_W), BF16),
        grid=(t // tm, MAIN_W // IN_TN),
        in_specs=[
            pl.BlockSpec((tm, D_MODEL), lambda i, j: (i, 0)),
            pl.BlockSpec((1, D_MODEL), lambda i, j: (0, 0)),
            pl.BlockSpec((D_MODEL, IN_TN), lambda i, j: (0, j)),
        ],
        out_specs=pl.BlockSpec((tm, IN_TN), lambda i, j: (i, j)),
        scratch_shapes=[pltpu.VMEM((tm, D_MODEL), BF16)],
        compiler_params=_params("parallel", "arbitrary"),
        name="inproj",
    )(x, g, w)


def _inproj_dil_kernel(x_ref, g_ref, w_ref, o1_ref, o2_ref, r_ref, *, tm):
    h = (_rms_scale(x_ref[...]) * g_ref[...]).astype(BF16)
    res = jnp.dot(h, w_ref[...], preferred_element_type=F32)
    for t, tile in enumerate(_lane_tiles(res)):
        r_ref[t] = tile
    tiles_per_group = DIL_W // LANES
    for g, o_ref in ((1, o1_ref), (2, o2_ref)):
        dil = DIL_PAIRS[g][1]
        for t in range(tiles_per_group):
            for r in range(dil):
                rows = r_ref[(g - 1) * tiles_per_group + t, pl.ds(r, tm // dil, stride=dil), :]
                o_ref[r, :, t * LANES:(t + 1) * LANES] = rows.astype(BF16)


def _inproj_dil(x, g, w, b, l):
    tm = min(IN_TM, l)
    nt = l // tm
    d1, d2 = DIL_PAIRS[1][1], DIL_PAIRS[2][1]
    return pl.pallas_call(
        functools.partial(_inproj_dil_kernel, tm=tm),
        out_shape=[jax.ShapeDtypeStruct((b, d1, l // d1, DIL_W), BF16),
                   jax.ShapeDtypeStruct((b, d2, l // d2, DIL_W), BF16)],
        grid=(b, nt),
        in_specs=[
            pl.BlockSpec((tm, D_MODEL), lambda bi, i: (bi * nt + i, 0)),
            pl.BlockSpec((1, D_MODEL), lambda bi, i: (0, 0)),
            pl.BlockSpec((D_MODEL, 2 * DIL_W), lambda bi, i: (0, 0)),
        ],
        out_specs=[pl.BlockSpec((None, d1, tm // d1, DIL_W), lambda bi, i: (bi, 0, i, 0)),
                   pl.BlockSpec((None, d2, tm // d2, DIL_W), lambda bi, i: (bi, 0, i, 0))],
        scratch_shapes=[pltpu.VMEM((2 * DIL_W // LANES, tm, LANES), F32)],
        compiler_params=_params("parallel", "parallel"),
        name="inproj_dil",
    )(x, g, w)


def _natten_bias(rpb):
    ri = np.arange(NA_Q_ROWS)[:, None]
    kr = np.arange(NA_K_ROWS)[None, :]
    c = np.arange(GRID_W)[:, None]
    kc = np.arange(GRID_W)[None, :]
    qstart = np.clip(c - NA_WIN_COLS // 2, 0, GRID_W - NA_WIN_COLS)
    col_ok = (kc >= qstart) & (kc < qstart + NA_WIN_COLS)
    col_idx = np.clip(kc - c + NA_WIN_COLS - 1, 0, 2 * NA_WIN_COLS - 2)
    col_sel = (col_idx[:, :, None] == np.arange(2 * NA_WIN_COLS - 1)) & col_ok[:, :, None]
    row_sel, row_ok = [], []
    for variant, q_minus_k_row0 in enumerate((0, NA_WIN_ROWS // 2, NA_WIN_ROWS)):
        first = (0 * ri, ri, 0 * ri + NA_K_ROWS - NA_WIN_ROWS)[variant]
        ok = (kr >= first) & (kr < first + NA_WIN_ROWS)
        idx = np.clip(kr - q_minus_k_row0 - ri + NA_WIN_ROWS - 1, 0, 2 * NA_WIN_ROWS - 2)
        row_sel.append((idx[:, :, None] == np.arange(2 * NA_WIN_ROWS - 1)) & ok[:, :, None])
        row_ok.append(ok)
    row_sel = jnp.asarray(np.stack(row_sel), F32)
    vals = jnp.einsum("vrka,ham,cjm->vhrckj", row_sel, rpb.astype(F32), jnp.asarray(col_sel, F32),
                      precision=lax.Precision.HIGHEST)
    ok = np.stack(row_ok)[:, None, :, None, :, None] & col_ok[None, None, None, :, None, :]
    vals = jnp.where(jnp.asarray(ok), vals * LOG2E, NEG_INF)
    return vals.reshape(3, NA_HEADS, NA_TQ, NA_TK)


def _natten_kernel(bias_ref, q_ref, k_ref, v_ref, z_ref, o_ref, *, rows):
    i = pl.program_id(1)
    krow0 = jnp.clip(i * NA_Q_ROWS - NA_WIN_ROWS // 2, 0, rows - NA_K_ROWS)
    kstart = pl.multiple_of(krow0 * GRID_W, GRID_W)
    lo, hi = _head_masks(1.0)
    lane = lax.broadcasted_iota(jnp.int32, (NA_TQ, LANES), 1)
    for p in range(NA_HEADS // 2):
        cols = slice(p * LANES, (p + 1) * LANES)
        q = q_ref[:, cols]
        k = k_ref[pl.ds(kstart, NA_TK), cols]
        v = v_ref[pl.ds(kstart, NA_TK), cols]
        halves = []
        for half, mask in enumerate((lo, hi)):
            s = lax.dot_general(q * mask, k, _NT, preferred_element_type=F32)
            s = s + bias_ref[2 * p + half]
            m = jnp.max(s, axis=-1, keepdims=True)
            e = jnp.exp2(s - m)
            l = jnp.sum(e, axis=-1, keepdims=True)
            o = jnp.dot(e.astype(BF16), v, preferred_element_type=F32)
            halves.append(o * (1.0 / l))
        y = jnp.where(lane < HEAD_DIM, halves[0], halves[1])
        o_ref[:, cols] = (y * _silu(z_ref[:, cols].astype(F32))).astype(BF16)


def _natten(u, bias, b, l):
    rows = l // GRID_W
    nblk = l // NA_TQ
    assert rows >= NA_K_ROWS and nblk >= 3

    def bias_map(bi, i):
        return (jnp.where(i == 0, 0, jnp.where(i == nblk - 1, 2, 1)), 0, 0, 0)

    return pl.pallas_call(
        functools.partial(_natten_kernel, rows=rows),
        out_shape=jax.ShapeDtypeStruct((b * l, A_W), BF16),
        grid=(b, nblk),
        in_specs=[
            pl.BlockSpec((None, NA_HEADS, NA_TQ, NA_TK), bias_map),
            pl.BlockSpec((NA_TQ, A_W), lambda bi, i: (bi * nblk + i, OFF_QA // A_W)),
            pl.BlockSpec((l, A_W), lambda bi, i: (bi, OFF_KA // A_W)),
            pl.BlockSpec((l, A_W), lambda bi, i: (bi, OFF_VA // A_W)),
            pl.BlockSpec((NA_TQ, A_W), lambda bi, i: (bi * nblk + i, OFF_ZA // A_W)),
        ],
        out_specs=pl.BlockSpec((NA_TQ, A_W), lambda bi, i: (bi * nblk + i, 0)),
        compiler_params=_params("parallel", "arbitrary"),
        name="natten",
    )(bias, u, u, u, u)


def _diff_slopes():
    return [float(np.float32(2.0 ** (-8.0 * (i + 1) / DIFF_HEADS))) for i in range(DIFF_HEADS)]


def _diff_bias_table(l, tq, tk):
    nvar = (l - tk) // tq + (l - tq) // tq + 1
    n0 = (l - tk) // tq
    delta = (jnp.arange(nvar, dtype=jnp.int32) - n0) * tq
    q_minus_k = jnp.arange(tq, dtype=jnp.int32)[:, None] - jnp.arange(tk, dtype=jnp.int32)[None, :]
    dist = jnp.abs(q_minus_k[None] + delta[:, None, None]).astype(F32)
    slopes = jnp.asarray(_diff_slopes(), F32) * LOG2E
    return -slopes[:, None, None, None] * dist[None], n0


def _diff_kernel(lq_ref, gd_ref, bias_ref, q_ref, k_ref, v_ref, z_ref, o_ref, s_ref, st_ref, *,
                 l, tq, tk, n0, lam_init):
    i = pl.program_id(2)
    nck = l // tk
    lq = lq_ref[...]
    lam = (jnp.exp(jnp.sum(lq[0:1] * lq[1:2], axis=-1, keepdims=True))
           - jnp.exp(jnp.sum(lq[2:3] * lq[3:4], axis=-1, keepdims=True)) + lam_init)
    lo, hi = _head_masks(1.0)
    q = q_ref[...]
    qs = (q * lo, q * hi)
    ntile = tk // LANES

    for c in range(nck):
        k = k_ref[c * tk:(c + 1) * tk, :]
        bias = bias_ref[i + (n0 - c * (tk // tq))]
        for n in range(2):
            s = lax.dot_general(qs[n], k, _NT, preferred_element_type=F32) + bias
            tiles = _lane_tiles(s)
            mloc = functools.reduce(jnp.maximum, tiles)
            es = [jnp.exp2(t - mloc) for t in tiles]
            for t in range(ntile):
                s_ref[n, c, :, t * LANES:(t + 1) * LANES] = es[t]
            st_ref[0, n, c] = mloc
            st_ref[1, n, c] = functools.reduce(jnp.add, es)

    coef = []
    for n in range(2):
        m = jnp.max(functools.reduce(jnp.maximum, [st_ref[0, n, c] for c in range(nck)]), axis=-1, keepdims=True)
        fs = [jnp.exp2(st_ref[0, n, c] - m) for c in range(nck)]
        den = jnp.sum(functools.reduce(jnp.add, [fs[c] * st_ref[1, n, c] for c in range(nck)]),
                      axis=-1, keepdims=True)
        scale = (1.0 / den) if n == 0 else (lam / den)
        coef.append([f * scale for f in fs])

    o = None
    for c in range(nck):
        w = jnp.concatenate(
            [s_ref[0, c, :, t * LANES:(t + 1) * LANES] * coef[0][c]
             - s_ref[1, c, :, t * LANES:(t + 1) * LANES] * coef[1][c] for t in range(ntile)], axis=-1)
        pv = jnp.dot(w.astype(BF16), v_ref[c * tk:(c + 1) * tk, :], preferred_element_type=F32)
        o = pv if o is None else o + pv

    y = (_rms_scale(o) * gd_ref[...]) * (1.0 - lam_init)
    o_ref[...] = (y * _silu(z_ref[...].astype(F32))).astype(BF16)


def _diff(u, lam_qk, g_diff, b, l, lam_init):
    tq = min(DIFF_TQ, l)
    tk = min(DIFF_TK, l)
    nq = l // tq
    qb, kb, vb, zb = (off // LANES for off in (OFF_QB, OFF_KB, OFF_VB, OFF_ZB))
    bias, n0 = _diff_bias_table(l, tq, tk)
    nvar = bias.shape[1]
    return pl.pallas_call(
        functools.partial(_diff_kernel, l=l, tq=tq, tk=tk, n0=n0, lam_init=lam_init),
        out_shape=jax.ShapeDtypeStruct((b * l, B_W), BF16),
        grid=(DIFF_HEADS, b, nq),
        in_specs=[
            pl.BlockSpec((4, HEAD_DIM), lambda h, bi, i: (0, 0)),
            pl.BlockSpec((1, 2 * HEAD_DIM), lambda h, bi, i: (0, 0)),
            pl.BlockSpec((None, nvar, tq, tk), lambda h, bi, i: (h, 0, 0, 0), pipeline_mode=pl.Buffered(1)),
            pl.BlockSpec((tq, LANES), lambda h, bi, i: (bi * nq + i, qb + h)),
            pl.BlockSpec((l, LANES), lambda h, bi, i: (bi, kb + h)),
            pl.BlockSpec((l, LANES), lambda h, bi, i: (bi, vb + h)),
            pl.BlockSpec((tq, LANES), lambda h, bi, i: (bi * nq + i, zb + h)),
        ],
        out_specs=pl.BlockSpec((tq, LANES), lambda h, bi, i: (bi * nq + i, h)),
        scratch_shapes=[pltpu.VMEM((2, l // tk, tq, tk), F32), pltpu.VMEM((2, 2, l // tk, tq, LANES), F32)],
        compiler_params=_params("parallel", "parallel", "arbitrary"),
        name="diffattn",
    )(lam_qk, g_diff, bias, u, u, u, u)


def _dil_slopes():
    n = DIL_GROUPS * DIL_HEADS
    return [float(np.float32(2.0 ** (-8.0 * (i + 1) / n))) for i in range(n)]


def _dilated_kernel(q_ref, k_ref, v_ref, o_ref, lse_ref, *, m, tq, tw, slopes):
    j = pl.program_id(2)
    q0pos = j * tq
    kstart = pl.multiple_of(jnp.clip(q0pos - DIL_SIDE, 0, m - tw), DIL_SIDE)
    rel = (kstart + lax.broadcasted_iota(jnp.int32, (tq, tw), 1)
           - q0pos - lax.broadcasted_iota(jnp.int32, (tq, tw), 0))
    dist = jnp.abs(rel).astype(F32)
    ok = dist <= float(DIL_SIDE)
    lo, hi = _head_masks(1.0)
    lane = lax.broadcasted_iota(jnp.int32, (tq, LANES), 1)
    for p in range(DIL_HEADS // 2):
        cols = slice(p * LANES, (p + 1) * LANES)
        q = q_ref[:, cols]
        k = k_ref[pl.ds(kstart, tw), cols]
        v = v_ref[pl.ds(kstart, tw), cols]
        outs, lses = [], []
        for half, mask in enumerate((lo, hi)):
            s = lax.dot_general(q * mask, k, _NT, preferred_element_type=F32)
            s = jnp.where(ok, s - slopes[2 * p + half] * dist, NEG_INF)
            mx = jnp.max(s, axis=-1, keepdims=True)
            e = jnp.exp2(s - mx)
            l = jnp.sum(e, axis=-1, keepdims=True)
            outs.append(jnp.dot(e.astype(BF16), v, preferred_element_type=F32) * (1.0 / l))
            lses.append(mx + jnp.log2(l))
        o_ref[:, cols] = jnp.where(lane < HEAD_DIM, outs[0], outs[1])
        lse_ref[:, cols] = jnp.where(lane < HEAD_DIM, lses[0], lses[1])


def _dilated_group(qkv, col0, b, l, g):
    dil = DIL_PAIRS[g][1]
    assert DIL_PAIRS[g][0] // (2 * dil) == DIL_SIDE
    m = l // dil
    tq = min(DIL_TQ, m)
    tw = min(tq + 2 * DIL_SIDE, m)
    slopes = tuple(s * dil * LOG2E for s in _dil_slopes()[g * DIL_HEADS:(g + 1) * DIL_HEADS])
    return pl.pallas_call(
        functools.partial(_dilated_kernel, m=m, tq=tq, tw=tw, slopes=slopes),
        out_shape=[jax.ShapeDtypeStruct((b, dil, m, C_W), F32)] * 2,
        grid=(b, dil, m // tq),
        in_specs=[
            pl.BlockSpec((None, None, tq, C_W), lambda bi, r, j: (bi, r, j, col0)),
            pl.BlockSpec((None, None, m, C_W), lambda bi, r, j: (bi, r, 0, col0 + 1)),
            pl.BlockSpec((None, None, m, C_W), lambda bi, r, j: (bi, r, 0, col0 + 2)),
        ],
        out_specs=[pl.BlockSpec((None, None, tq, C_W), lambda bi, r, j: (bi, r, j, 0))] * 2,
        compiler_params=_params("parallel", "parallel", "arbitrary"),
        name=f"dilated{g}",
    )(qkv, qkv, qkv)


def _merge_kernel(x_ref, ga_ref, gb_ref, o0_ref, l0_ref, o1_ref, l1_ref, o2_ref, l2_ref, zc_ref, gl_ref,
                  bg_ref, wa_ref, wb_ref, wc_ref, wo_ref, gf_ref, out_ref, *scratch, tm, final_norm):
    outs, lses = [o0_ref[0]], [l0_ref[0]]
    for g, refs in ((1, (o1_ref, l1_ref)), (2, (o2_ref, l2_ref))):
        dil = DIL_PAIRS[g][1]
        for src, dst in zip(refs, scratch[2 * (g - 1):2 * g]):
            for t in range(C_W // LANES):
                for r in range(dil):
                    dst[t, pl.ds(r, tm // dil, stride=dil), :] = src[r, :, t * LANES:(t + 1) * LANES]
        outs.append(jnp.concatenate([scratch[2 * (g - 1)][t] for t in range(C_W // LANES)], axis=-1))
        lses.append(jnp.concatenate([scratch[2 * (g - 1) + 1][t] for t in range(C_W // LANES)], axis=-1))
    mx = jnp.maximum(jnp.maximum(lses[0], lses[1]), lses[2])
    wts = [jnp.exp2(x - mx) for x in lses]
    den = wts[0] + wts[1] + wts[2]
    yc = (wts[0] * outs[0] + wts[1] * outs[1] + wts[2] * outs[2]) / den
    gc = (yc * _silu(zc_ref[...].astype(F32))).astype(BF16)
    branches = (jnp.dot(ga_ref[...], wa_ref[...], preferred_element_type=F32),
                jnp.dot(gb_ref[...], wb_ref[...], preferred_element_type=F32),
                jnp.dot(gc, wc_ref[...], preferred_element_type=F32))
    merged = None
    for n, pn in enumerate(branches):
        logit = gl_ref[:, n * D_MODEL:(n + 1) * D_MODEL].astype(F32) + bg_ref[n:n + 1, :]
        term = pn / (1.0 + jnp.exp(-logit))
        merged = term if merged is None else merged + term
    y = x_ref[...] + jnp.dot(merged.astype(BF16), wo_ref[...], preferred_element_type=F32)
    if final_norm:
        y = _rms_scale(y) * gf_ref[...]
    out_ref[...] = y


def _merge(x, ga, gb, dil, u, b_gate, wa, wb, wc, wo, g_final, b, l, final_norm):
    tm = min(MERGE_TM, l)
    nt = l // tm
    row = lambda w, col=0: pl.BlockSpec((tm, w), lambda bi, i: (bi * nt + i, col))
    full = lambda a: pl.BlockSpec(a.shape, lambda bi, i: (0, 0))
    cm = lambda g: pl.BlockSpec((None, DIL_PAIRS[g][1], tm // DIL_PAIRS[g][1], C_W), lambda bi, i: (bi, 0, i, 0))
    dil_specs = [cm(g) for g in range(DIL_GROUPS) for _ in range(2)]
    dil_args = [a for pair in dil for a in pair]
    return pl.pallas_call(
        functools.partial(_merge_kernel, tm=tm, final_norm=final_norm),
        out_shape=jax.ShapeDtypeStruct((b * l, D_MODEL), F32),
        grid=(b, nt),
        in_specs=[row(D_MODEL), row(A_W), row(B_W)] + dil_specs
                 + [row(C_W, OFF_ZC // C_W), row(GL_W, OFF_GL // GL_W)]
                 + [full(a) for a in (b_gate, wa, wb, wc, wo, g_final)],
        out_specs=row(D_MODEL),
        scratch_shapes=[pltpu.VMEM((C_W // LANES, tm, LANES), F32)] * 4,
        compiler_params=_params("parallel", "parallel"),
        name="merge",
    )(x, ga, gb, *dil_args, u, u, b_gate, wa, wb, wc, wo, g_final)


def _trunk(x, layers, g_final):
    b, l, _ = x.shape
    x = x.reshape(b * l, D_MODEL)
    for n, p in enumerate(layers):
        lam_init = 0.8 - 0.6 * math.exp(-0.3 * n)
        u = _inproj(x, p["g_norm"], p["w_main"])
        c1, c2 = _inproj_dil(x, p["g_norm"], p["w_dil"], b, l)
        ga = _natten(u, p["na_bias"], b, l)
        gb = _diff(u, p["lam_qk"], p["g_diff"], b, l, lam_init)
        dil = [_dilated_group(u.reshape(b, 1, l, MAIN_W), OFF_C0 // C_W, b, l, 0),
               _dilated_group(c1, 0, b, l, 1),
               _dilated_group(c2, 0, b, l, 2)]
        x = _merge(x, ga, gb, dil, u, p["b_gate"], p["w_br_a"], p["w_br_b"], p["w_br_c"], p["w_out"],
                   g_final, b, l, final_norm=(n == len(layers) - 1))
    return x.reshape(b, l, D_MODEL)


def _prepare_layers(g_norm, w_in, b_gate, rpb, lam_qk, g_diff, w_br_a, w_br_b, w_br_c, w_out):
    segments = _column_segments()
    layers = []
    for n in range(w_in.shape[0]):
        w = jnp.concatenate([w_in[n][:, s:s + width] * scale for s, width, scale in segments], axis=1).astype(BF16)
        layers.append(dict(
            g_norm=g_norm[n].reshape(1, D_MODEL).astype(F32),
            w_main=w[:, :MAIN_W],
            w_dil=w[:, MAIN_W:],
            b_gate=b_gate[n].astype(F32),
            na_bias=_natten_bias(rpb[n]),
            lam_qk=lam_qk[n].astype(F32),
            g_diff=g_diff[n].reshape(1, 2 * HEAD_DIM).astype(F32),
            w_br_a=w_br_a[n].astype(BF16),
            w_br_b=w_br_b[n].astype(BF16),
            w_br_c=w_br_c[n].astype(BF16),
            w_out=w_out[n].astype(BF16),
        ))
    return layers


def kernel(x_prompt, x_sample, g_norm, w_in, b_gate, rpb, lam_qk, g_diff, w_br_a, w_br_b, w_br_c, w_out, g_final):
    layers = _prepare_layers(g_norm, w_in, b_gate, rpb, lam_qk, g_diff, w_br_a, w_br_b, w_br_c, w_out)
    gf = g_final.reshape(1, D_MODEL).astype(F32)
    return (_trunk(x_prompt, layers, gf), _trunk(x_sample, layers, gf))
```

```python
import functools
import math

import numpy as np
import jax
import jax.numpy as jnp
from jax import lax
from jax.experimental import pallas as pl
from jax.experimental.pallas import tpu as pltpu

F32 = jnp.float32
BF16 = jnp.bfloat16

D_MODEL = 1024
GRID_W = 64
HEAD_DIM = 64
RMS_EPS = 1e-6
NEG_INF = -1e30
LANES = 128

NA_HEADS = 8
NA_WIN_ROWS = 8
NA_WIN_COLS = 16
NA_Q_ROWS = 4
NA_K_ROWS = NA_Q_ROWS + NA_WIN_ROWS
NA_TQ = NA_Q_ROWS * GRID_W
NA_TK = NA_K_ROWS * GRID_W

DIFF_HEADS = 4
DIFF_TQ = 256
DIFF_TK = 512

DIL_PAIRS = ((128, 1), (512, 4), (2048, 16))
DIL_GROUPS = 3
DIL_HEADS = 4
DIL_SIDE = 64
DIL_TQ = 256
DIL_UNITS = 2

A_W = NA_HEADS * HEAD_DIM
B_W = DIFF_HEADS * 2 * HEAD_DIM
C_QKV = DIL_GROUPS * DIL_HEADS * HEAD_DIM
C_W = DIL_HEADS * HEAD_DIM
N_BRANCH = 3
GL_W = N_BRANCH * D_MODEL
IN_WIDTH = 4 * A_W + 4 * B_W + 3 * C_QKV + C_W + GL_W

OFF_GL = 0
OFF_QA = GL_W
OFF_KA = OFF_QA + A_W
OFF_VA = OFF_KA + A_W
OFF_ZA = OFF_VA + A_W
OFF_QB = OFF_ZA + A_W
OFF_KB = OFF_QB + B_W
OFF_VB = OFF_KB + B_W
OFF_ZB = OFF_VB + B_W
OFF_ZC = OFF_ZB + B_W
OFF_C0 = OFF_ZC + C_W
MAIN_W = OFF_C0 + 3 * C_W
DIL_W = 3 * C_W
assert MAIN_W + 2 * DIL_W == IN_WIDTH

VMEM_LIMIT = 56 * 1024 * 1024

IN_TM = 1024
IN_TN = MAIN_W // 4
MERGE_TM = 512

_NT = (((1,), (1,)), ((), ()))
LOG2E = math.log2(math.e)
Q_SCALE = HEAD_DIM ** -0.5 * LOG2E


def _column_segments():
    b0 = 4 * A_W
    c0 = b0 + 4 * B_W
    zc0 = c0 + 3 * C_QKV
    qkv = lambda g: [(c0 + n * C_QKV + g * C_W, C_W, Q_SCALE if n == 0 else 1.0) for n in range(3)]
    return ([(zc0 + C_W, GL_W, 1.0), (0, A_W, Q_SCALE), (A_W, 3 * A_W, 1.0), (b0, B_W, Q_SCALE),
             (b0 + B_W, 3 * B_W, 1.0), (zc0, C_W, 1.0)] + qkv(0) + qkv(1) + qkv(2))


def _params(*sem):
    return pltpu.CompilerParams(dimension_semantics=sem, vmem_limit_bytes=VMEM_LIMIT)


def _silu(z):
    return z / (1.0 + jnp.exp(-z))


def _rms_scale(x):
    return x * lax.rsqrt(jnp.mean(x * x, axis=-1, keepdims=True) + RMS_EPS)


def _head_masks(scale):
    lane = lax.broadcasted_iota(jnp.int32, (1, LANES), 1)
    lo = jnp.where(lane < HEAD_DIM, scale, 0.0).astype(BF16)
    hi = jnp.where(lane >= HEAD_DIM, scale, 0.0).astype(BF16)
    return lo, hi


def _lane_tiles(x):
    if x.shape[-1] <= LANES:
        return [x]
    return [x[:, n * LANES:(n + 1) * LANES] for n in range(x.shape[-1] // LANES)]


def _inproj_kernel(x_ref, g_ref, w_ref, o_ref, h_ref):
    @pl.when(pl.program_id(1) == 0)
    def _():
        h_ref[...] = (_rms_scale(x_ref[...]) * g_ref[...]).astype(BF16)

    o_ref[...] = jnp.dot(h_ref[...], w_ref[...], preferred_element_type=F32).astype(BF16)


def _inproj(x, g, w):
    t = x.shape[0]
    tm = min(IN_TM, t)
    return pl.pallas_call(
        _inproj_kernel,
        out_shape=jax.ShapeDtypeStruct((t, MAIN_W), BF16),
        grid=(t // tm, MAIN_W // IN_TN),
        in_specs=[
            pl.BlockSpec((tm, D_MODEL), lambda i, j: (i, 0)),
            pl.BlockSpec((1, D_MODEL), lambda i, j: (0, 0)),
            pl.BlockSpec((D_MODEL, IN_TN), lambda i, j: (0, j)),
        ],
        out_specs=pl.BlockSpec((tm, IN_TN), lambda i, j: (i, j)),
        scratch_shapes=[pltpu.VMEM((tm, D_MODEL), BF16)],
        compiler_params=_params("parallel", "arbitrary"),
        name="inproj",
    )(x, g, w)


def _inproj_dil_kernel(x_ref, g_ref, w_ref, o1_ref, o2_ref, r_ref, *, tm):
    h = (_rms_scale(x_ref[...]) * g_ref[...]).astype(BF16)
    res = jnp.dot(h, w_ref[...], preferred_element_type=F32)
    for t, tile in enumerate(_lane_tiles(res)):
        r_ref[t] = tile
    tiles_per_group = DIL_W // LANES
    for g, o_ref in ((1, o1_ref), (2, o2_ref)):
        dil = DIL_PAIRS[g][1]
        for t in range(tiles_per_group):
            for r in range(dil):
                rows = r_ref[(g - 1) * tiles_per_group + t, pl.ds(r, tm // dil, stride=dil), :]
                o_ref[r, :, t * LANES:(t + 1) * LANES] = rows.astype(BF16)


def _inproj_dil(x, g, w, b, l):
    tm = min(IN_TM, l)
    nt = l // tm
    d1, d2 = DIL_PAIRS[1][1], DIL_PAIRS[2][1]
    return pl.pallas_call(
        functools.partial(_inproj_dil_kernel, tm=tm),
        out_shape=[jax.ShapeDtypeStruct((b, d1, l // d1, DIL_W), BF16),
                   jax.ShapeDtypeStruct((b, d2, l // d2, DIL_W), BF16)],
        grid=(b, nt),
        in_specs=[
            pl.BlockSpec((tm, D_MODEL), lambda bi, i: (bi * nt + i, 0)),
            pl.BlockSpec((1, D_MODEL), lambda bi, i: (0, 0)),
            pl.BlockSpec((D_MODEL, 2 * DIL_W), lambda bi, i: (0, 0)),
        ],
        out_specs=[pl.BlockSpec((None, d1, tm // d1, DIL_W), lambda bi, i: (bi, 0, i, 0)),
                   pl.BlockSpec((None, d2, tm // d2, DIL_W), lambda bi, i: (bi, 0, i, 0))],
        scratch_shapes=[pltpu.VMEM((2 * DIL_W // LANES, tm, LANES), F32)],
        compiler_params=_params("parallel", "parallel"),
        name="inproj_dil",
    )(x, g, w)


def _natten_bias(rpb):
    ri = np.arange(NA_Q_ROWS)[:, None]
    kr = np.arange(NA_K_ROWS)[None, :]
    c = np.arange(GRID_W)[:, None]
    kc = np.arange(GRID_W)[None, :]
    qstart = np.clip(c - NA_WIN_COLS // 2, 0, GRID_W - NA_WIN_COLS)
    col_ok = (kc >= qstart) & (kc < qstart + NA_WIN_COLS)
    col_idx = np.clip(kc - c + NA_WIN_COLS - 1, 0, 2 * NA_WIN_COLS - 2)
    col_sel = (col_idx[:, :, None] == np.arange(2 * NA_WIN_COLS - 1)) & col_ok[:, :, None]
    row_sel, row_ok = [], []
    for variant, q_minus_k_row0 in enumerate((0, NA_WIN_ROWS // 2, NA_WIN_ROWS)):
        first = (0 * ri, ri, 0 * ri + NA_K_ROWS - NA_WIN_ROWS)[variant]
        ok = (kr >= first) & (kr < first + NA_WIN_ROWS)
        idx = np.clip(kr - q_minus_k_row0 - ri + NA_WIN_ROWS - 1, 0, 2 * NA_WIN_ROWS - 2)
        row_sel.append((idx[:, :, None] == np.arange(2 * NA_WIN_ROWS - 1)) & ok[:, :, None])
        row_ok.append(ok)
    row_sel = jnp.asarray(np.stack(row_sel), F32)
    vals = jnp.einsum("vrka,ham,cjm->vhrckj", row_sel, rpb.astype(F32), jnp.asarray(col_sel, F32),
                      precision=lax.Precision.HIGHEST)
    ok = np.stack(row_ok)[:, None, :, None, :, None] & col_ok[None, None, None, :, None, :]
    vals = jnp.where(jnp.asarray(ok), vals * LOG2E, NEG_INF)
    return vals.reshape(3, NA_HEADS, NA_TQ, NA_TK)


def _softmax_pv_phases(score_fns, value_fns, s_ref):
    heads = range(len(score_fns))
    lane_max = []
    for h in heads:
        s = score_fns[h]()
        s_ref[h] = s
        lane_max.append(functools.reduce(jnp.maximum, _lane_tiles(s)))
    row_max = [jnp.max(x, axis=-1, keepdims=True) for x in lane_max]
    pv, lane_sum = [], []
    for h in heads:
        e = jnp.exp2(s_ref[h] - row_max[h])
        lane_sum.append(functools.reduce(jnp.add, _lane_tiles(e)))
        pv.append(jnp.dot(e.astype(BF16), value_fns[h](), preferred_element_type=F32))
    row_sum = [jnp.sum(x, axis=-1, keepdims=True) for x in lane_sum]
    return pv, row_max, row_sum


def _natten_kernel(bias_ref, q_ref, k_ref, v_ref, z_ref, o_ref, s_ref, *, rows):
    i = pl.program_id(1)
    krow0 = jnp.clip(i * NA_Q_ROWS - NA_WIN_ROWS // 2, 0, rows - NA_K_ROWS)
    kstart = pl.multiple_of(krow0 * GRID_W, GRID_W)
    masks = _head_masks(1.0)
    lane = lax.broadcasted_iota(jnp.int32, (NA_TQ, LANES), 1)

    def score_fn(h):
        cols = slice(h // 2 * LANES, (h // 2 + 1) * LANES)
        return lambda: (lax.dot_general(q_ref[:, cols] * masks[h % 2], k_ref[pl.ds(kstart, NA_TK), cols], _NT,
                                        preferred_element_type=F32) + bias_ref[h])

    def value_fn(h):
        cols = slice(h // 2 * LANES, (h // 2 + 1) * LANES)
        return lambda: v_ref[pl.ds(kstart, NA_TK), cols]

    pv, _, row_sum = _softmax_pv_phases([score_fn(h) for h in range(NA_HEADS)],
                                        [value_fn(h) for h in range(NA_HEADS)], s_ref)
    for p in range(NA_HEADS // 2):
        cols = slice(p * LANES, (p + 1) * LANES)
        y = jnp.where(lane < HEAD_DIM, pv[2 * p] * (1.0 / row_sum[2 * p]), pv[2 * p + 1] * (1.0 / row_sum[2 * p + 1]))
        o_ref[:, cols] = (y * _silu(z_ref[:, cols].astype(F32))).astype(BF16)


def _natten(u, bias, b, l):
    rows = l // GRID_W
    nblk = l // NA_TQ
    assert rows >= NA_K_ROWS and nblk >= 3

    def bias_map(bi, i):
        return (jnp.where(i == 0, 0, jnp.where(i == nblk - 1, 2, 1)), 0, 0, 0)

    return pl.pallas_call(
        functools.partial(_natten_kernel, rows=rows),
        out_shape=jax.ShapeDtypeStruct((b * l, A_W), BF16),
        grid=(b, nblk),
        in_specs=[
            pl.BlockSpec((None, NA_HEADS, NA_TQ, NA_TK), bias_map),
            pl.BlockSpec((NA_TQ, A_W), lambda bi, i: (bi * nblk + i, OFF_QA // A_W)),
            pl.BlockSpec((l, A_W), lambda bi, i: (bi, OFF_KA // A_W)),
            pl.BlockSpec((l, A_W), lambda bi, i: (bi, OFF_VA // A_W)),
            pl.BlockSpec((NA_TQ, A_W), lambda bi, i: (bi * nblk + i, OFF_ZA // A_W)),
        ],
        out_specs=pl.BlockSpec((NA_TQ, A_W), lambda bi, i: (bi * nblk + i, 0)),
        scratch_shapes=[pltpu.VMEM((NA_HEADS, NA_TQ, NA_TK), F32)],
        compiler_params=_params("parallel", "arbitrary"),
        name="natten",
    )(bias, u, u, u, u)


def _diff_slopes():
    return [float(np.float32(2.0 ** (-8.0 * (i + 1) / DIFF_HEADS))) for i in range(DIFF_HEADS)]


def _diff_bias_table(l, tq, tk):
    nvar = (l - tk) // tq + (l - tq) // tq + 1
    n0 = (l - tk) // tq
    delta = (jnp.arange(nvar, dtype=jnp.int32) - n0) * tq
    q_minus_k = jnp.arange(tq, dtype=jnp.int32)[:, None] - jnp.arange(tk, dtype=jnp.int32)[None, :]
    dist = jnp.abs(q_minus_k[None] + delta[:, None, None]).astype(F32)
    slopes = jnp.asarray(_diff_slopes(), F32) * LOG2E
    return -slopes[:, None, None, None] * dist[None], n0


def _diff_kernel(lq_ref, gd_ref, bias_ref, q_ref, k_ref, v_ref, z_ref, o_ref, s_ref, st_ref, *,
                 l, tq, tk, n0, lam_init):
    i = pl.program_id(2)
    nck = l // tk
    lq = lq_ref[...]
    lam = (jnp.exp(jnp.sum(lq[0:1] * lq[1:2], axis=-1, keepdims=True))
           - jnp.exp(jnp.sum(lq[2:3] * lq[3:4], axis=-1, keepdims=True)) + lam_init)
    lo, hi = _head_masks(1.0)
    q = q_ref[...]
    qs = (q * lo, q * hi)
    ntile = tk // LANES

    for c in range(nck):
        k = k_ref[c * tk:(c + 1) * tk, :]
        bias = bias_ref[i + (n0 - c * (tk // tq))]
        for n in range(2):
            s = lax.dot_general(qs[n], k, _NT, preferred_element_type=F32) + bias
            tiles = _lane_tiles(s)
            mloc = functools.reduce(jnp.maximum, tiles)
            es = [jnp.exp2(t - mloc) for t in tiles]
            for t in range(ntile):
                s_ref[n, c, :, t * LANES:(t + 1) * LANES] = es[t]
            st_ref[0, n, c] = mloc
            st_ref[1, n, c] = functools.reduce(jnp.add, es)

    coef = []
    for n in range(2):
        m = jnp.max(functools.reduce(jnp.maximum, [st_ref[0, n, c] for c in range(nck)]), axis=-1, keepdims=True)
        fs = [jnp.exp2(st_ref[0, n, c] - m) for c in range(nck)]
        den = jnp.sum(functools.reduce(jnp.add, [fs[c] * st_ref[1, n, c] for c in range(nck)]),
                      axis=-1, keepdims=True)
        scale = (1.0 / den) if n == 0 else (lam / den)
        coef.append([f * scale for f in fs])

    o = None
    for c in range(nck):
        w = jnp.concatenate(
            [s_ref[0, c, :, t * LANES:(t + 1) * LANES] * coef[0][c]
             - s_ref[1, c, :, t * LANES:(t + 1) * LANES] * coef[1][c] for t in range(ntile)], axis=-1)
        pv = jnp.dot(w.astype(BF16), v_ref[c * tk:(c + 1) * tk, :], preferred_element_type=F32)
        o = pv if o is None else o + pv

    y = (_rms_scale(o) * gd_ref[...]) * (1.0 - lam_init)
    o_ref[...] = (y * _silu(z_ref[...].astype(F32))).astype(BF16)


def _diff(u, lam_qk, g_diff, b, l, lam_init):
    tq = min(DIFF_TQ, l)
    tk = min(DIFF_TK, l)
    nq = l // tq
    qb, kb, vb, zb = (off // LANES for off in (OFF_QB, OFF_KB, OFF_VB, OFF_ZB))
    bias, n0 = _diff_bias_table(l, tq, tk)
    nvar = bias.shape[1]
    return pl.pallas_call(
        functools.partial(_diff_kernel, l=l, tq=tq, tk=tk, n0=n0, lam_init=lam_init),
        out_shape=jax.ShapeDtypeStruct((b * l, B_W), BF16),
        grid=(DIFF_HEADS, b, nq),
        in_specs=[
            pl.BlockSpec((4, HEAD_DIM), lambda h, bi, i: (0, 0)),
            pl.BlockSpec((1, 2 * HEAD_DIM), lambda h, bi, i: (0, 0)),
            pl.BlockSpec((None, nvar, tq, tk), lambda h, bi, i: (h, 0, 0, 0), pipeline_mode=pl.Buffered(1)),
            pl.BlockSpec((tq, LANES), lambda h, bi, i: (bi * nq + i, qb + h)),
            pl.BlockSpec((l, LANES), lambda h, bi, i: (bi, kb + h)),
            pl.BlockSpec((l, LANES), lambda h, bi, i: (bi, vb + h)),
            pl.BlockSpec((tq, LANES), lambda h, bi, i: (bi * nq + i, zb + h)),
        ],
        out_specs=pl.BlockSpec((tq, LANES), lambda h, bi, i: (bi * nq + i, h)),
        scratch_shapes=[pltpu.VMEM((2, l // tk, tq, tk), F32), pltpu.VMEM((2, 2, l // tk, tq, LANES), F32)],
        compiler_params=_params("parallel", "parallel", "arbitrary"),
        name="diffattn",
    )(lam_qk, g_diff, bias, u, u, u, u)


def _dil_slopes():
    n = DIL_GROUPS * DIL_HEADS
    return [float(np.float32(2.0 ** (-8.0 * (i + 1) / n))) for i in range(n)]


def _dilated_kernel(q_ref, k_ref, v_ref, o_ref, lse_ref, s_ref, *, m, tq, tw, nclass, nsub, slopes):
    j = pl.program_id(2)
    masks = _head_masks(1.0)
    lane = lax.broadcasted_iota(jnp.int32, (tq, LANES), 1)
    units = [(r, sb) for r in range(nclass) for sb in range(nsub)]
    kstarts, dists = [], []
    for sb in range(nsub):
        q0pos = (j * nsub + sb) * tq
        kstart = pl.multiple_of(jnp.clip(q0pos - DIL_SIDE, 0, m - tw), DIL_SIDE)
        rel = (kstart + lax.broadcasted_iota(jnp.int32, (tq, tw), 1)
               - q0pos - lax.broadcasted_iota(jnp.int32, (tq, tw), 0))
        kstarts.append(kstart)
        dists.append(jnp.abs(rel).astype(F32))

    def score_fn(r, sb, h):
        cols = slice(h // 2 * LANES, (h // 2 + 1) * LANES)

        def fn():
            q = q_ref[r, sb * tq:(sb + 1) * tq, cols] * masks[h % 2]
            s = lax.dot_general(q, k_ref[r, pl.ds(kstarts[sb], tw), cols], _NT, preferred_element_type=F32)
            return jnp.where(dists[sb] <= float(DIL_SIDE), s - slopes[h] * dists[sb], NEG_INF)
        return fn

    def value_fn(r, sb, h):
        cols = slice(h // 2 * LANES, (h // 2 + 1) * LANES)
        return lambda: v_ref[r, pl.ds(kstarts[sb], tw), cols]

    heads = [(r, sb, h) for r, sb in units for h in range(DIL_HEADS)]
    pv, row_max, row_sum = _softmax_pv_phases([score_fn(*x) for x in heads], [value_fn(*x) for x in heads], s_ref)
    for n, (r, sb) in enumerate(units):
        for p in range(DIL_HEADS // 2):
            a, b = n * DIL_HEADS + 2 * p, n * DIL_HEADS + 2 * p + 1
            rows, cols = slice(sb * tq, (sb + 1) * tq), slice(p * LANES, (p + 1) * LANES)
            o_ref[r, rows, cols] = jnp.where(lane < HEAD_DIM, pv[a] * (1.0 / row_sum[a]), pv[b] * (1.0 / row_sum[b]))
            lse_ref[r, rows, cols] = jnp.where(lane < HEAD_DIM, row_max[a] + jnp.log2(row_sum[a]),
                                               row_max[b] + jnp.log2(row_sum[b]))


def _dilated_group(qkv, col0, b, l, g):
    dil = DIL_PAIRS[g][1]
    assert DIL_PAIRS[g][0] // (2 * dil) == DIL_SIDE
    m = l // dil
    tq = min(DIL_TQ, m)
    tw = min(tq + 2 * DIL_SIDE, m)
    nsub = min(DIL_UNITS, m // tq)
    nclass = min(DIL_UNITS // nsub, dil)
    slopes = tuple(s * dil * LOG2E for s in _dil_slopes()[g * DIL_HEADS:(g + 1) * DIL_HEADS])
    return pl.pallas_call(
        functools.partial(_dilated_kernel, m=m, tq=tq, tw=tw, nclass=nclass, nsub=nsub, slopes=slopes),
        out_shape=[jax.ShapeDtypeStruct((b, dil, m, C_W), F32)] * 2,
        grid=(b, dil // nclass, m // (nsub * tq)),
        in_specs=[
            pl.BlockSpec((None, nclass, nsub * tq, C_W), lambda bi, r, j: (bi, r, j, col0)),
            pl.BlockSpec((None, nclass, m, C_W), lambda bi, r, j: (bi, r, 0, col0 + 1)),
            pl.BlockSpec((None, nclass, m, C_W), lambda bi, r, j: (bi, r, 0, col0 + 2)),
        ],
        out_specs=[pl.BlockSpec((None, nclass, nsub * tq, C_W), lambda bi, r, j: (bi, r, j, 0))] * 2,
        scratch_shapes=[pltpu.VMEM((nclass * nsub * DIL_HEADS, tq, tw), F32)],
        compiler_params=_params("parallel", "parallel", "arbitrary"),
        name=f"dilated{g}",
    )(qkv, qkv, qkv)


def _merge_kernel(x_ref, ga_ref, gb_ref, o0_ref, l0_ref, o1_ref, l1_ref, o2_ref, l2_ref, zc_ref, gl_ref,
                  bg_ref, wa_ref, wb_ref, wc_ref, wo_ref, gf_ref, out_ref, *scratch, tm, final_norm):
    outs, lses = [o0_ref[0]], [l0_ref[0]]
    for g, refs in ((1, (o1_ref, l1_ref)), (2, (o2_ref, l2_ref))):
        dil = DIL_PAIRS[g][1]
        for src, dst in zip(refs, scratch[2 * (g - 1):2 * g]):
            for t in range(C_W // LANES):
                for r in range(dil):
                    dst[t, pl.ds(r, tm // dil, stride=dil), :] = src[r, :, t * LANES:(t + 1) * LANES]
        outs.append(jnp.concatenate([scratch[2 * (g - 1)][t] for t in range(C_W // LANES)], axis=-1))
        lses.append(jnp.concatenate([scratch[2 * (g - 1) + 1][t] for t in range(C_W // LANES)], axis=-1))
    mx = jnp.maximum(jnp.maximum(lses[0], lses[1]), lses[2])
    wts = [jnp.exp2(x - mx) for x in lses]
    den = wts[0] + wts[1] + wts[2]
    yc = (wts[0] * outs[0] + wts[1] * outs[1] + wts[2] * outs[2]) / den
    gc = (yc * _silu(zc_ref[...].astype(F32))).astype(BF16)
    branches = (jnp.dot(ga_ref[...], wa_ref[...], preferred_element_type=F32),
                jnp.dot(gb_ref[...], wb_ref[...], preferred_element_type=F32),
                jnp.dot(gc, wc_ref[...], preferred_element_type=F32))
    merged = None
    for n, pn in enumerate(branches):
        logit = gl_ref[:, n * D_MODEL:(n + 1) * D_MODEL].astype(F32) + bg_ref[n:n + 1, :]
        term = pn / (1.0 + jnp.exp(-logit))
        merged = term if merged is None else merged + term
    y = x_ref[...] + jnp.dot(merged.astype(BF16), wo_ref[...], preferred_element_type=F32)
    if final_norm:
        y = _rms_scale(y) * gf_ref[...]
    out_ref[...] = y


def _merge(x, ga, gb, dil, u, b_gate, wa, wb, wc, wo, g_final, b, l, final_norm):
    tm = min(MERGE_TM, l)
    nt = l // tm
    row = lambda w, col=0: pl.BlockSpec((tm, w), lambda bi, i: (bi * nt + i, col))
    full = lambda a: pl.BlockSpec(a.shape, lambda bi, i: (0, 0))
    cm = lambda g: pl.BlockSpec((None, DIL_PAIRS[g][1], tm // DIL_PAIRS[g][1], C_W), lambda bi, i: (bi, 0, i, 0))
    dil_specs = [cm(g) for g in range(DIL_GROUPS) for _ in range(2)]
    dil_args = [a for pair in dil for a in pair]
    return pl.pallas_call(
        functools.partial(_merge_kernel, tm=tm, final_norm=final_norm),
        out_shape=jax.ShapeDtypeStruct((b * l, D_MODEL), F32),
        grid=(b, nt),
        in_specs=[row(D_MODEL), row(A_W), row(B_W)] + dil_specs
                 + [row(C_W, OFF_ZC // C_W), row(GL_W, OFF_GL // GL_W)]
                 + [full(a) for a in (b_gate, wa, wb, wc, wo, g_final)],
        out_specs=row(D_MODEL),
        scratch_shapes=[pltpu.VMEM((C_W // LANES, tm, LANES), F32)] * 4,
        compiler_params=_params("parallel", "parallel"),
        name="merge",
    )(x, ga, gb, *dil_args, u, u, b_gate, wa, wb, wc, wo, g_final)


def _trunk(x, layers, g_final):
    b, l, _ = x.shape
    x = x.reshape(b * l, D_MODEL)
    for n, p in enumerate(layers):
        lam_init = 0.8 - 0.6 * math.exp(-0.3 * n)
        u = _inproj(x, p["g_norm"], p["w_main"])
        c1, c2 = _inproj_dil(x, p["g_norm"], p["w_dil"], b, l)
        ga = _natten(u, p["na_bias"], b, l)
        gb = _diff(u, p["lam_qk"], p["g_diff"], b, l, lam_init)
        dil = [_dilated_group(u.reshape(b, 1, l, MAIN_W), OFF_C0 // C_W, b, l, 0),
               _dilated_group(c1, 0, b, l, 1),
               _dilated_group(c2, 0, b, l, 2)]
        x = _merge(x, ga, gb, dil, u, p["b_gate"], p["w_br_a"], p["w_br_b"], p["w_br_c"], p["w_out"],
                   g_final, b, l, final_norm=(n == len(layers) - 1))
    return x.reshape(b, l, D_MODEL)


def _prepare_layers(g_norm, w_in, b_gate, rpb, lam_qk, g_diff, w_br_a, w_br_b, w_br_c, w_out):
    segments = _column_segments()
    layers = []
    for n in range(w_in.shape[0]):
        w = jnp.concatenate([w_in[n][:, s:s + width] * scale for s, width, scale in segments], axis=1).astype(BF16)
        layers.append(dict(
            g_norm=g_norm[n].reshape(1, D_MODEL).astype(F32),
            w_main=w[:, :MAIN_W],
            w_dil=w[:, MAIN_W:],
            b_gate=b_gate[n].astype(F32),
            na_bias=_natten_bias(rpb[n]),
            lam_qk=lam_qk[n].astype(F32),
            g_diff=g_diff[n].reshape(1, 2 * HEAD_DIM).astype(F32),
            w_br_a=w_br_a[n].astype(BF16),
            w_br_b=w_br_b[n].astype(BF16),
            w_br_c=w_br_c[n].astype(BF16),
            w_out=w_out[n].astype(BF16),
        ))
    return layers


def kernel(x_prompt, x_sample, g_norm, w_in, b_gate, rpb, lam_qk, g_diff, w_br_a, w_br_b, w_br_c, w_out, g_final):
    layers = _prepare_layers(g_norm, w_in, b_gate, rpb, lam_qk, g_diff, w_br_a, w_br_b, w_br_c, w_out)
    gf = g_final.reshape(1, D_MODEL).astype(F32)
    return (_trunk(x_prompt, layers, gf), _trunk(x_sample, layers, gf))
```

```python
import functools
import math

import numpy as np
import jax
import jax.numpy as jnp
from jax import lax
from jax.experimental import pallas as pl
from jax.experimental.pallas import tpu as pltpu

F32 = jnp.float32
BF16 = jnp.bfloat16

D_MODEL = 1024
GRID_W = 64
HEAD_DIM = 64
RMS_EPS = 1e-6
NEG_INF = -1e30
LANES = 128

NA_HEADS = 8
NA_WIN_ROWS = 8
NA_WIN_COLS = 16
NA_Q_ROWS = 4
NA_K_ROWS = NA_Q_ROWS + NA_WIN_ROWS
NA_TQ = NA_Q_ROWS * GRID_W
NA_TK = NA_K_ROWS * GRID_W

DIFF_HEADS = 4
DIFF_TQ = 256
DIFF_TK = 512
DIFF_BLOCKS = 2

DIL_PAIRS = ((128, 1), (512, 4), (2048, 16))
DIL_GROUPS = 3
DIL_HEADS = 4
DIL_SIDE = 64
DIL_TQ = 256
DIL_UNITS = 2

A_W = NA_HEADS * HEAD_DIM
B_W = DIFF_HEADS * 2 * HEAD_DIM
C_QKV = DIL_GROUPS * DIL_HEADS * HEAD_DIM
C_W = DIL_HEADS * HEAD_DIM
N_BRANCH = 3
GL_W = N_BRANCH * D_MODEL
IN_WIDTH = 4 * A_W + 4 * B_W + 3 * C_QKV + C_W + GL_W

OFF_GL = 0
OFF_QA = GL_W
OFF_KA = OFF_QA + A_W
OFF_VA = OFF_KA + A_W
OFF_ZA = OFF_VA + A_W
OFF_QB = OFF_ZA + A_W
OFF_KB = OFF_QB + B_W
OFF_VB = OFF_KB + B_W
OFF_ZB = OFF_VB + B_W
OFF_ZC = OFF_ZB + B_W
OFF_C0 = OFF_ZC + C_W
MAIN_W = OFF_C0 + 3 * C_W
DIL_W = 3 * C_W
assert MAIN_W + 2 * DIL_W == IN_WIDTH

VMEM_LIMIT = 56 * 1024 * 1024

IN_TM = 1024
IN_TN = MAIN_W // 4
MERGE_TM = 512

_NT = (((1,), (1,)), ((), ()))
LOG2E = math.log2(math.e)
Q_SCALE = HEAD_DIM ** -0.5 * LOG2E


def _column_segments():
    b0 = 4 * A_W
    c0 = b0 + 4 * B_W
    zc0 = c0 + 3 * C_QKV
    qkv = lambda g: [(c0 + n * C_QKV + g * C_W, C_W, Q_SCALE if n == 0 else 1.0) for n in range(3)]
    return ([(zc0 + C_W, GL_W, 1.0), (0, A_W, Q_SCALE), (A_W, 3 * A_W, 1.0), (b0, B_W, Q_SCALE),
             (b0 + B_W, 3 * B_W, 1.0), (zc0, C_W, 1.0)] + qkv(0) + qkv(1) + qkv(2))


def _params(*sem):
    return pltpu.CompilerParams(dimension_semantics=sem, vmem_limit_bytes=VMEM_LIMIT)


def _silu(z):
    return z / (1.0 + jnp.exp(-z))


def _rms_scale(x):
    return x * lax.rsqrt(jnp.mean(x * x, axis=-1, keepdims=True) + RMS_EPS)


def _head_masks(scale):
    lane = lax.broadcasted_iota(jnp.int32, (1, LANES), 1)
    lo = jnp.where(lane < HEAD_DIM, scale, 0.0).astype(BF16)
    hi = jnp.where(lane >= HEAD_DIM, scale, 0.0).astype(BF16)
    return lo, hi


def _lane_tiles(x):
    if x.shape[-1] <= LANES:
        return [x]
    return [x[:, n * LANES:(n + 1) * LANES] for n in range(x.shape[-1] // LANES)]


def _inproj_kernel(x_ref, g_ref, w_ref, o_ref, h_ref):
    @pl.when(pl.program_id(1) == 0)
    def _():
        h_ref[...] = (_rms_scale(x_ref[...]) * g_ref[...]).astype(BF16)

    o_ref[...] = jnp.dot(h_ref[...], w_ref[...], preferred_element_type=F32).astype(BF16)


def _inproj(x, g, w):
    t = x.shape[0]
    tm = min(IN_TM, t)
    return pl.pallas_call(
        _inproj_kernel,
        out_shape=jax.ShapeDtypeStruct((t, MAIN_W), BF16),
        grid=(t // tm, MAIN_W // IN_TN),
        in_specs=[
            pl.BlockSpec((tm, D_MODEL), lambda i, j: (i, 0)),
            pl.BlockSpec((1, D_MODEL), lambda i, j: (0, 0)),
            pl.BlockSpec((D_MODEL, IN_TN), lambda i, j: (0, j)),
        ],
        out_specs=pl.BlockSpec((tm, IN_TN), lambda i, j: (i, j)),
        scratch_shapes=[pltpu.VMEM((tm, D_MODEL), BF16)],
        compiler_params=_params("parallel", "arbitrary"),
        name="inproj",
    )(x, g, w)


def _inproj_dil_kernel(x_ref, g_ref, w_ref, o1_ref, o2_ref, r_ref, *, tm):
    h = (_rms_scale(x_ref[...]) * g_ref[...]).astype(BF16)
    res = jnp.dot(h, w_ref[...], preferred_element_type=F32)
    for t, tile in enumerate(_lane_tiles(res)):
        r_ref[t] = tile
    tiles_per_group = DIL_W // LANES
    for g, o_ref in ((1, o1_ref), (2, o2_ref)):
        dil = DIL_PAIRS[g][1]
        for t in range(tiles_per_group):
            for r in range(dil):
                rows = r_ref[(g - 1) * tiles_per_group + t, pl.ds(r, tm // dil, stride=dil), :]
                o_ref[r, :, t * LANES:(t + 1) * LANES] = rows.astype(BF16)


def _inproj_dil(x, g, w, b, l):
    tm = min(IN_TM, l)
    nt = l // tm
    d1, d2 = DIL_PAIRS[1][1], DIL_PAIRS[2][1]
    return pl.pallas_call(
        functools.partial(_inproj_dil_kernel, tm=tm),
        out_shape=[jax.ShapeDtypeStruct((b, d1, l // d1, DIL_W), BF16),
                   jax.ShapeDtypeStruct((b, d2, l // d2, DIL_W), BF16)],
        grid=(b, nt),
        in_specs=[
            pl.BlockSpec((tm, D_MODEL), lambda bi, i: (bi * nt + i, 0)),
            pl.BlockSpec((1, D_MODEL), lambda bi, i: (0, 0)),
            pl.BlockSpec((D_MODEL, 2 * DIL_W), lambda bi, i: (0, 0)),
        ],
        out_specs=[pl.BlockSpec((None, d1, tm // d1, DIL_W), lambda bi, i: (bi, 0, i, 0)),
                   pl.BlockSpec((None, d2, tm // d2, DIL_W), lambda bi, i: (bi, 0, i, 0))],
        scratch_shapes=[pltpu.VMEM((2 * DIL_W // LANES, tm, LANES), F32)],
        compiler_params=_params("parallel", "parallel"),
        name="inproj_dil",
    )(x, g, w)


def _natten_bias(rpb):
    ri = np.arange(NA_Q_ROWS)[:, None]
    kr = np.arange(NA_K_ROWS)[None, :]
    c = np.arange(GRID_W)[:, None]
    kc = np.arange(GRID_W)[None, :]
    qstart = np.clip(c - NA_WIN_COLS // 2, 0, GRID_W - NA_WIN_COLS)
    col_ok = (kc >= qstart) & (kc < qstart + NA_WIN_COLS)
    col_idx = np.clip(kc - c + NA_WIN_COLS - 1, 0, 2 * NA_WIN_COLS - 2)
    col_sel = (col_idx[:, :, None] == np.arange(2 * NA_WIN_COLS - 1)) & col_ok[:, :, None]
    row_sel, row_ok = [], []
    for variant, q_minus_k_row0 in enumerate((0, NA_WIN_ROWS // 2, NA_WIN_ROWS)):
        first = (0 * ri, ri, 0 * ri + NA_K_ROWS - NA_WIN_ROWS)[variant]
        ok = (kr >= first) & (kr < first + NA_WIN_ROWS)
        idx = np.clip(kr - q_minus_k_row0 - ri + NA_WIN_ROWS - 1, 0, 2 * NA_WIN_ROWS - 2)
        row_sel.append((idx[:, :, None] == np.arange(2 * NA_WIN_ROWS - 1)) & ok[:, :, None])
        row_ok.append(ok)
    row_sel = jnp.asarray(np.stack(row_sel), F32)
    vals = jnp.einsum("vrka,ham,cjm->vhrckj", row_sel, rpb.astype(F32), jnp.asarray(col_sel, F32),
                      precision=lax.Precision.HIGHEST)
    ok = np.stack(row_ok)[:, None, :, None, :, None] & col_ok[None, None, None, :, None, :]
    vals = jnp.where(jnp.asarray(ok), vals * LOG2E, NEG_INF)
    return vals.reshape(3, NA_HEADS, NA_TQ, NA_TK)


def _softmax_pv_phases(score_fns, value_fns, s_ref):
    heads = range(len(score_fns))
    lane_max = []
    for h in heads:
        s = score_fns[h]()
        s_ref[h] = s
        lane_max.append(functools.reduce(jnp.maximum, _lane_tiles(s)))
    row_max = [jnp.max(x, axis=-1, keepdims=True) for x in lane_max]
    pv, lane_sum = [], []
    for h in heads:
        e = jnp.exp2(s_ref[h] - row_max[h])
        lane_sum.append(functools.reduce(jnp.add, _lane_tiles(e)))
        pv.append(jnp.dot(e.astype(BF16), value_fns[h](), preferred_element_type=F32))
    row_sum = [jnp.sum(x, axis=-1, keepdims=True) for x in lane_sum]
    return pv, row_max, row_sum


def _natten_kernel(bias_ref, q_ref, k_ref, v_ref, z_ref, o_ref, s_ref, *, rows):
    i = pl.program_id(1)
    krow0 = jnp.clip(i * NA_Q_ROWS - NA_WIN_ROWS // 2, 0, rows - NA_K_ROWS)
    kstart = pl.multiple_of(krow0 * GRID_W, GRID_W)
    masks = _head_masks(1.0)
    lane = lax.broadcasted_iota(jnp.int32, (NA_TQ, LANES), 1)

    def score_fn(h):
        cols = slice(h // 2 * LANES, (h // 2 + 1) * LANES)
        return lambda: (lax.dot_general(q_ref[:, cols] * masks[h % 2], k_ref[pl.ds(kstart, NA_TK), cols], _NT,
                                        preferred_element_type=F32) + bias_ref[h])

    def value_fn(h):
        cols = slice(h // 2 * LANES, (h // 2 + 1) * LANES)
        return lambda: v_ref[pl.ds(kstart, NA_TK), cols]

    pv, _, row_sum = _softmax_pv_phases([score_fn(h) for h in range(NA_HEADS)],
                                        [value_fn(h) for h in range(NA_HEADS)], s_ref)
    for p in range(NA_HEADS // 2):
        cols = slice(p * LANES, (p + 1) * LANES)
        y = jnp.where(lane < HEAD_DIM, pv[2 * p] * (1.0 / row_sum[2 * p]), pv[2 * p + 1] * (1.0 / row_sum[2 * p + 1]))
        o_ref[:, cols] = (y * _silu(z_ref[:, cols].astype(F32))).astype(BF16)


def _natten(u, bias, b, l):
    rows = l // GRID_W
    nblk = l // NA_TQ
    assert rows >= NA_K_ROWS and nblk >= 3

    def bias_map(bi, i):
        return (jnp.where(i == 0, 0, jnp.where(i == nblk - 1, 2, 1)), 0, 0, 0)

    return pl.pallas_call(
        functools.partial(_natten_kernel, rows=rows),
        out_shape=jax.ShapeDtypeStruct((b * l, A_W), BF16),
        grid=(b, nblk),
        in_specs=[
            pl.BlockSpec((None, NA_HEADS, NA_TQ, NA_TK), bias_map),
            pl.BlockSpec((NA_TQ, A_W), lambda bi, i: (bi * nblk + i, OFF_QA // A_W)),
            pl.BlockSpec((l, A_W), lambda bi, i: (bi, OFF_KA // A_W)),
            pl.BlockSpec((l, A_W), lambda bi, i: (bi, OFF_VA // A_W)),
            pl.BlockSpec((NA_TQ, A_W), lambda bi, i: (bi * nblk + i, OFF_ZA // A_W)),
        ],
        out_specs=pl.BlockSpec((NA_TQ, A_W), lambda bi, i: (bi * nblk + i, 0)),
        scratch_shapes=[pltpu.VMEM((NA_HEADS, NA_TQ, NA_TK), F32)],
        compiler_params=_params("parallel", "arbitrary"),
        name="natten",
    )(bias, u, u, u, u)


def _diff_slopes():
    return [float(np.float32(2.0 ** (-8.0 * (i + 1) / DIFF_HEADS))) for i in range(DIFF_HEADS)]


def _diff_bias_table(l, tq, tk):
    nvar = (l - tk) // tq + (l - tq) // tq + 1
    n0 = (l - tk) // tq
    delta = (jnp.arange(nvar, dtype=jnp.int32) - n0) * tq
    q_minus_k = jnp.arange(tq, dtype=jnp.int32)[:, None] - jnp.arange(tk, dtype=jnp.int32)[None, :]
    dist = jnp.abs(q_minus_k[None] + delta[:, None, None]).astype(F32)
    slopes = jnp.asarray(_diff_slopes(), F32) * LOG2E
    return -slopes[:, None, None, None] * dist[None], n0


def _diff_kernel(lq_ref, gd_ref, bias_ref, q_ref, k_ref, v_ref, z_ref, o_ref, s_ref, st_ref, *,
                 l, tq, tk, nblk, n0, lam_init):
    i = pl.program_id(2)
    nck = l // tk
    ntile = tk // LANES
    lq = lq_ref[...]
    lam = (jnp.exp(jnp.sum(lq[0:1] * lq[1:2], axis=-1, keepdims=True))
           - jnp.exp(jnp.sum(lq[2:3] * lq[3:4], axis=-1, keepdims=True)) + lam_init)
    masks = _head_masks(1.0)
    blocks = range(nblk)

    for blk in blocks:
        q = q_ref[blk * tq:(blk + 1) * tq, :]
        for c in range(nck):
            k = k_ref[c * tk:(c + 1) * tk, :]
            bias = bias_ref[i * nblk + blk + (n0 - c * (tk // tq))]
            for n in range(2):
                s = lax.dot_general(q * masks[n], k, _NT, preferred_element_type=F32) + bias
                tiles = _lane_tiles(s)
                mloc = functools.reduce(jnp.maximum, tiles)
                es = [jnp.exp2(t - mloc) for t in tiles]
                for t in range(ntile):
                    s_ref[blk, n, c, :, t * LANES:(t + 1) * LANES] = es[t]
                st_ref[blk, 0, n, c] = mloc
                st_ref[blk, 1, n, c] = functools.reduce(jnp.add, es)

    for blk in blocks:
        coef = []
        for n in range(2):
            m = jnp.max(functools.reduce(jnp.maximum, [st_ref[blk, 0, n, c] for c in range(nck)]),
                        axis=-1, keepdims=True)
            fs = [jnp.exp2(st_ref[blk, 0, n, c] - m) for c in range(nck)]
            den = jnp.sum(functools.reduce(jnp.add, [fs[c] * st_ref[blk, 1, n, c] for c in range(nck)]),
                          axis=-1, keepdims=True)
            scale = (1.0 / den) if n == 0 else (lam / den)
            coef.append([f * scale for f in fs])

        o = None
        for c in range(nck):
            w = jnp.concatenate(
                [s_ref[blk, 0, c, :, t * LANES:(t + 1) * LANES] * coef[0][c]
                 - s_ref[blk, 1, c, :, t * LANES:(t + 1) * LANES] * coef[1][c] for t in range(ntile)], axis=-1)
            pv = jnp.dot(w.astype(BF16), v_ref[c * tk:(c + 1) * tk, :], preferred_element_type=F32)
            o = pv if o is None else o + pv

        y = (_rms_scale(o) * gd_ref[...]) * (1.0 - lam_init)
        rows = slice(blk * tq, (blk + 1) * tq)
        o_ref[rows, :] = (y * _silu(z_ref[rows, :].astype(F32))).astype(BF16)


def _diff(u, lam_qk, g_diff, b, l, lam_init):
    tq = min(DIFF_TQ, l)
    tk = min(DIFF_TK, l)
    nblk = min(DIFF_BLOCKS, l // tq)
    nq = l // (nblk * tq)
    qb, kb, vb, zb = (off // LANES for off in (OFF_QB, OFF_KB, OFF_VB, OFF_ZB))
    bias, n0 = _diff_bias_table(l, tq, tk)
    nvar = bias.shape[1]
    return pl.pallas_call(
        functools.partial(_diff_kernel, l=l, tq=tq, tk=tk, nblk=nblk, n0=n0, lam_init=lam_init),
        out_shape=jax.ShapeDtypeStruct((b * l, B_W), BF16),
        grid=(DIFF_HEADS, b, nq),
        in_specs=[
            pl.BlockSpec((4, HEAD_DIM), lambda h, bi, i: (0, 0)),
            pl.BlockSpec((1, 2 * HEAD_DIM), lambda h, bi, i: (0, 0)),
            pl.BlockSpec((None, nvar, tq, tk), lambda h, bi, i: (h, 0, 0, 0), pipeline_mode=pl.Buffered(1)),
            pl.BlockSpec((nblk * tq, LANES), lambda h, bi, i: (bi * nq + i, qb + h)),
            pl.BlockSpec((l, LANES), lambda h, bi, i: (bi, kb + h)),
            pl.BlockSpec((l, LANES), lambda h, bi, i: (bi, vb + h)),
            pl.BlockSpec((nblk * tq, LANES), lambda h, bi, i: (bi * nq + i, zb + h)),
        ],
        out_specs=pl.BlockSpec((nblk * tq, LANES), lambda h, bi, i: (bi * nq + i, h)),
        scratch_shapes=[pltpu.VMEM((nblk, 2, l // tk, tq, tk), F32),
                        pltpu.VMEM((nblk, 2, 2, l // tk, tq, LANES), F32)],
        compiler_params=_params("parallel", "parallel", "arbitrary"),
        name="diffattn",
    )(lam_qk, g_diff, bias, u, u, u, u)


def _dil_slopes():
    n = DIL_GROUPS * DIL_HEADS
    return [float(np.float32(2.0 ** (-8.0 * (i + 1) / n))) for i in range(n)]


def _dilated_kernel(q_ref, k_ref, v_ref, o_ref, lse_ref, s_ref, *, m, tq, tw, nclass, nsub, slopes):
    j = pl.program_id(2)
    masks = _head_masks(1.0)
    lane = lax.broadcasted_iota(jnp.int32, (tq, LANES), 1)
    units = [(r, sb) for r in range(nclass) for sb in range(nsub)]
    kstarts, dists = [], []
    for sb in range(nsub):
        q0pos = (j * nsub + sb) * tq
        kstart = pl.multiple_of(jnp.clip(q0pos - DIL_SIDE, 0, m - tw), DIL_SIDE)
        rel = (kstart + lax.broadcasted_iota(jnp.int32, (tq, tw), 1)
               - q0pos - lax.broadcasted_iota(jnp.int32, (tq, tw), 0))
        kstarts.append(kstart)
        dists.append(jnp.abs(rel).astype(F32))

    def score_fn(r, sb, h):
        cols = slice(h // 2 * LANES, (h // 2 + 1) * LANES)

        def fn():
            q = q_ref[r, sb * tq:(sb + 1) * tq, cols] * masks[h % 2]
            s = lax.dot_general(q, k_ref[r, pl.ds(kstarts[sb], tw), cols], _NT, preferred_element_type=F32)
            return jnp.where(dists[sb] <= float(DIL_SIDE), s - slopes[h] * dists[sb], NEG_INF)
        return fn

    def value_fn(r, sb, h):
        cols = slice(h // 2 * LANES, (h // 2 + 1) * LANES)
        return lambda: v_ref[r, pl.ds(kstarts[sb], tw), cols]

    heads = [(r, sb, h) for r, sb in units for h in range(DIL_HEADS)]
    pv, row_max, row_sum = _softmax_pv_phases([score_fn(*x) for x in heads], [value_fn(*x) for x in heads], s_ref)
    for n, (r, sb) in enumerate(units):
        for p in range(DIL_HEADS // 2):
            a, b = n * DIL_HEADS + 2 * p, n * DIL_HEADS + 2 * p + 1
            rows, cols = slice(sb * tq, (sb + 1) * tq), slice(p * LANES, (p + 1) * LANES)
            o_ref[r, rows, cols] = jnp.where(lane < HEAD_DIM, pv[a] * (1.0 / row_sum[a]), pv[b] * (1.0 / row_sum[b]))
            lse_ref[r, rows, cols] = jnp.where(lane < HEAD_DIM, row_max[a] + jnp.log2(row_sum[a]),
                                               row_max[b] + jnp.log2(row_sum[b]))


def _dilated_group(qkv, col0, b, l, g):
    dil = DIL_PAIRS[g][1]
    assert DIL_PAIRS[g][0] // (2 * dil) == DIL_SIDE
    m = l // dil
    tq = min(DIL_TQ, m)
    tw = min(tq + 2 * DIL_SIDE, m)
    nsub = min(DIL_UNITS, m // tq)
    nclass = min(DIL_UNITS // nsub, dil)
    slopes = tuple(s * dil * LOG2E for s in _dil_slopes()[g * DIL_HEADS:(g + 1) * DIL_HEADS])
    return pl.pallas_call(
        functools.partial(_dilated_kernel, m=m, tq=tq, tw=tw, nclass=nclass, nsub=nsub, slopes=slopes),
        out_shape=[jax.ShapeDtypeStruct((b, dil, m, C_W), F32)] * 2,
        grid=(b, dil // nclass, m // (nsub * tq)),
        in_specs=[
            pl.BlockSpec((None, nclass, nsub * tq, C_W), lambda bi, r, j: (bi, r, j, col0)),
            pl.BlockSpec((None, nclass, m, C_W), lambda bi, r, j: (bi, r, 0, col0 + 1)),
            pl.BlockSpec((None, nclass, m, C_W), lambda bi, r, j: (bi, r, 0, col0 + 2)),
        ],
        out_specs=[pl.BlockSpec((None, nclass, nsub * tq, C_W), lambda bi, r, j: (bi, r, j, 0))] * 2,
        scratch_shapes=[pltpu.VMEM((nclass * nsub * DIL_HEADS, tq, tw), F32)],
        compiler_params=_params("parallel", "parallel", "arbitrary"),
        name=f"dilated{g}",
    )(qkv, qkv, qkv)


def _merge_kernel(x_ref, ga_ref, gb_ref, o0_ref, l0_ref, o1_ref, l1_ref, o2_ref, l2_ref, zc_ref, gl_ref,
                  bg_ref, wa_ref, wb_ref, wc_ref, wo_ref, gf_ref, out_ref, *scratch, tm, final_norm):
    outs, lses = [o0_ref[0]], [l0_ref[0]]
    for g, refs in ((1, (o1_ref, l1_ref)), (2, (o2_ref, l2_ref))):
        dil = DIL_PAIRS[g][1]
        for src, dst in zip(refs, scratch[2 * (g - 1):2 * g]):
            for t in range(C_W // LANES):
                for r in range(dil):
                    dst[t, pl.ds(r, tm // dil, stride=dil), :] = src[r, :, t * LANES:(t + 1) * LANES]
        outs.append(jnp.concatenate([scratch[2 * (g - 1)][t] for t in range(C_W // LANES)], axis=-1))
        lses.append(jnp.concatenate([scratch[2 * (g - 1) + 1][t] for t in range(C_W // LANES)], axis=-1))
    mx = jnp.maximum(jnp.maximum(lses[0], lses[1]), lses[2])
    wts = [jnp.exp2(x - mx) for x in lses]
    den = wts[0] + wts[1] + wts[2]
    yc = (wts[0] * outs[0] + wts[1] * outs[1] + wts[2] * outs[2]) / den
    gc = (yc * _silu(zc_ref[...].astype(F32))).astype(BF16)
    branches = (jnp.dot(ga_ref[...], wa_ref[...], preferred_element_type=F32),
                jnp.dot(gb_ref[...], wb_ref[...], preferred_element_type=F32),
                jnp.dot(gc, wc_ref[...], preferred_element_type=F32))
    merged = None
    for n, pn in enumerate(branches):
        logit = gl_ref[:, n * D_MODEL:(n + 1) * D_MODEL].astype(F32) + bg_ref[n:n + 1, :]
        term = pn / (1.0 + jnp.exp(-logit))
        merged = term if merged is None else merged + term
    y = x_ref[...] + jnp.dot(merged.astype(BF16), wo_ref[...], preferred_element_type=F32)
    if final_norm:
        y = _rms_scale(y) * gf_ref[...]
    out_ref[...] = y


def _merge(x, ga, gb, dil, u, b_gate, wa, wb, wc, wo, g_final, b, l, final_norm):
    tm = min(MERGE_TM, l)
    nt = l // tm
    row = lambda w, col=0: pl.BlockSpec((tm, w), lambda bi, i: (bi * nt + i, col))
    full = lambda a: pl.BlockSpec(a.shape, lambda bi, i: (0, 0))
    cm = lambda g: pl.BlockSpec((None, DIL_PAIRS[g][1], tm // DIL_PAIRS[g][1], C_W), lambda bi, i: (bi, 0, i, 0))
    dil_specs = [cm(g) for g in range(DIL_GROUPS) for _ in range(2)]
    dil_args = [a for pair in dil for a in pair]
    return pl.pallas_call(
        functools.partial(_merge_kernel, tm=tm, final_norm=final_norm),
        out_shape=jax.ShapeDtypeStruct((b * l, D_MODEL), F32),
        grid=(b, nt),
        in_specs=[row(D_MODEL), row(A_W), row(B_W)] + dil_specs
                 + [row(C_W, OFF_ZC // C_W), row(GL_W, OFF_GL // GL_W)]
                 + [full(a) for a in (b_gate, wa, wb, wc, wo, g_final)],
        out_specs=row(D_MODEL),
        scratch_shapes=[pltpu.VMEM((C_W // LANES, tm, LANES), F32)] * 4,
        compiler_params=_params("parallel", "parallel"),
        name="merge",
    )(x, ga, gb, *dil_args, u, u, b_gate, wa, wb, wc, wo, g_final)


def _trunk(x, layers, g_final):
    b, l, _ = x.shape
    x = x.reshape(b * l, D_MODEL)
    for n, p in enumerate(layers):
        lam_init = 0.8 - 0.6 * math.exp(-0.3 * n)
        u = _inproj(x, p["g_norm"], p["w_main"])
        c1, c2 = _inproj_dil(x, p["g_norm"], p["w_dil"], b, l)
        ga = _natten(u, p["na_bias"], b, l)
        gb = _diff(u, p["lam_qk"], p["g_diff"], b, l, lam_init)
        dil = [_dilated_group(u.reshape(b, 1, l, MAIN_W), OFF_C0 // C_W, b, l, 0),
               _dilated_group(c1, 0, b, l, 1),
               _dilated_group(c2, 0, b, l, 2)]
        x = _merge(x, ga, gb, dil, u, p["b_gate"], p["w_br_a"], p["w_br_b"], p["w_br_c"], p["w_out"],
                   g_final, b, l, final_norm=(n == len(layers) - 1))
    return x.reshape(b, l, D_MODEL)


def _prepare_layers(g_norm, w_in, b_gate, rpb, lam_qk, g_diff, w_br_a, w_br_b, w_br_c, w_out):
    segments = _column_segments()
    layers = []
    for n in range(w_in.shape[0]):
        w = jnp.concatenate([w_in[n][:, s:s + width] * scale for s, width, scale in segments], axis=1).astype(BF16)
        layers.append(dict(
            g_norm=g_norm[n].reshape(1, D_MODEL).astype(F32),
            w_main=w[:, :MAIN_W],
            w_dil=w[:, MAIN_W:],
            b_gate=b_gate[n].astype(F32),
            na_bias=_natten_bias(rpb[n]),
            lam_qk=lam_qk[n].astype(F32),
            g_diff=g_diff[n].reshape(1, 2 * HEAD_DIM).astype(F32),
            w_br_a=w_br_a[n].astype(BF16),
            w_br_b=w_br_b[n].astype(BF16),
            w_br_c=w_br_c[n].astype(BF16),
            w_out=w_out[n].astype(BF16),
        ))
    return layers


def kernel(x_prompt, x_sample, g_norm, w_in, b_gate, rpb, lam_qk, g_diff, w_br_a, w_br_b, w_br_c, w_out, g_final):
    layers = _prepare_layers(g_norm, w_in, b_gate, rpb, lam_qk, g_diff, w_br_a, w_br_b, w_br_c, w_out)
    gf = g_final.reshape(1, D_MODEL).astype(F32)
    return (_trunk(x_prompt, layers, gf), _trunk(x_sample, layers, gf))
```

```python
import functools
import math

import numpy as np
import jax
import jax.numpy as jnp
from jax import lax
from jax.experimental import pallas as pl
from jax.experimental.pallas import tpu as pltpu

F32 = jnp.float32
BF16 = jnp.bfloat16

D_MODEL = 1024
GRID_W = 64
HEAD_DIM = 64
RMS_EPS = 1e-6
NEG_INF = -1e30
LANES = 128

NA_HEADS = 8
NA_WIN_ROWS = 8
NA_WIN_COLS = 16
NA_Q_ROWS = 4
NA_K_ROWS = NA_Q_ROWS + NA_WIN_ROWS
NA_TQ = NA_Q_ROWS * GRID_W
NA_TK = NA_K_ROWS * GRID_W

DIFF_HEADS = 4
DIFF_TQ = 256
DIFF_TK = 512
DIFF_BLOCKS = 2

DIL_PAIRS = ((128, 1), (512, 4), (2048, 16))
DIL_GROUPS = 3
DIL_HEADS = 4
DIL_SIDE = 64
DIL_TQ = 256
DIL_UNITS = 4

A_W = NA_HEADS * HEAD_DIM
B_W = DIFF_HEADS * 2 * HEAD_DIM
C_QKV = DIL_GROUPS * DIL_HEADS * HEAD_DIM
C_W = DIL_HEADS * HEAD_DIM
N_BRANCH = 3
GL_W = N_BRANCH * D_MODEL
IN_WIDTH = 4 * A_W + 4 * B_W + 3 * C_QKV + C_W + GL_W

OFF_GL = 0
OFF_QA = GL_W
OFF_KA = OFF_QA + A_W
OFF_VA = OFF_KA + A_W
OFF_ZA = OFF_VA + A_W
OFF_QB = OFF_ZA + A_W
OFF_KB = OFF_QB + B_W
OFF_VB = OFF_KB + B_W
OFF_ZB = OFF_VB + B_W
OFF_ZC = OFF_ZB + B_W
OFF_C0 = OFF_ZC + C_W
MAIN_W = OFF_C0 + 3 * C_W
DIL_W = 3 * C_W
assert MAIN_W + 2 * DIL_W == IN_WIDTH

VMEM_LIMIT = 56 * 1024 * 1024

IN_TM = 1024
IN_TN = MAIN_W // 2
MERGE_TM = 512

_NT = (((1,), (1,)), ((), ()))
LOG2E = math.log2(math.e)
Q_SCALE = HEAD_DIM ** -0.5 * LOG2E


def _column_segments():
    b0 = 4 * A_W
    c0 = b0 + 4 * B_W
    zc0 = c0 + 3 * C_QKV
    qkv = lambda g: [(c0 + n * C_QKV + g * C_W, C_W, Q_SCALE if n == 0 else 1.0) for n in range(3)]
    return ([(zc0 + C_W, GL_W, 1.0), (0, A_W, Q_SCALE), (A_W, 3 * A_W, 1.0), (b0, B_W, Q_SCALE),
             (b0 + B_W, 3 * B_W, 1.0), (zc0, C_W, 1.0)] + qkv(0) + qkv(1) + qkv(2))


def _params(*sem):
    return pltpu.CompilerParams(dimension_semantics=sem, vmem_limit_bytes=VMEM_LIMIT)


def _silu(z):
    return z / (1.0 + jnp.exp(-z))


def _rms_scale(x):
    return x * lax.rsqrt(jnp.mean(x * x, axis=-1, keepdims=True) + RMS_EPS)


def _head_masks(scale):
    lane = lax.broadcasted_iota(jnp.int32, (1, LANES), 1)
    lo = jnp.where(lane < HEAD_DIM, scale, 0.0).astype(BF16)
    hi = jnp.where(lane >= HEAD_DIM, scale, 0.0).astype(BF16)
    return lo, hi


def _lane_tiles(x):
    if x.shape[-1] <= LANES:
        return [x]
    return [x[:, n * LANES:(n + 1) * LANES] for n in range(x.shape[-1] // LANES)]


def _inproj_kernel(x_ref, g_ref, w_ref, o_ref, h_ref):
    @pl.when(pl.program_id(1) == 0)
    def _():
        h_ref[...] = (_rms_scale(x_ref[...]) * g_ref[...]).astype(BF16)

    o_ref[...] = jnp.dot(h_ref[...], w_ref[...], preferred_element_type=F32).astype(BF16)


def _inproj(x, g, w):
    t = x.shape[0]
    tm = min(IN_TM, t)
    return pl.pallas_call(
        _inproj_kernel,
        out_shape=jax.ShapeDtypeStruct((t, MAIN_W), BF16),
        grid=(t // tm, MAIN_W // IN_TN),
        in_specs=[
            pl.BlockSpec((tm, D_MODEL), lambda i, j: (i, 0)),
            pl.BlockSpec((1, D_MODEL), lambda i, j: (0, 0)),
            pl.BlockSpec((D_MODEL, IN_TN), lambda i, j: (0, j)),
        ],
        out_specs=pl.BlockSpec((tm, IN_TN), lambda i, j: (i, j)),
        scratch_shapes=[pltpu.VMEM((tm, D_MODEL), BF16)],
        compiler_params=_params("parallel", "arbitrary"),
        name="inproj",
    )(x, g, w)


def _inproj_dil_kernel(x_ref, g_ref, w_ref, o1_ref, o2_ref, r_ref, *, tm):
    h = (_rms_scale(x_ref[...]) * g_ref[...]).astype(BF16)
    res = jnp.dot(h, w_ref[...], preferred_element_type=F32)
    for t, tile in enumerate(_lane_tiles(res)):
        r_ref[t] = tile
    tiles_per_group = DIL_W // LANES
    for g, o_ref in ((1, o1_ref), (2, o2_ref)):
        dil = DIL_PAIRS[g][1]
        for t in range(tiles_per_group):
            for r in range(dil):
                rows = r_ref[(g - 1) * tiles_per_group + t, pl.ds(r, tm // dil, stride=dil), :]
                o_ref[r, :, t * LANES:(t + 1) * LANES] = rows.astype(BF16)


def _inproj_dil(x, g, w, b, l):
    tm = min(IN_TM, l)
    nt = l // tm
    d1, d2 = DIL_PAIRS[1][1], DIL_PAIRS[2][1]
    return pl.pallas_call(
        functools.partial(_inproj_dil_kernel, tm=tm),
        out_shape=[jax.ShapeDtypeStruct((b, d1, l // d1, DIL_W), BF16),
                   jax.ShapeDtypeStruct((b, d2, l // d2, DIL_W), BF16)],
        grid=(b, nt),
        in_specs=[
            pl.BlockSpec((tm, D_MODEL), lambda bi, i: (bi * nt + i, 0)),
            pl.BlockSpec((1, D_MODEL), lambda bi, i: (0, 0)),
            pl.BlockSpec((D_MODEL, 2 * DIL_W), lambda bi, i: (0, 0)),
        ],
        out_specs=[pl.BlockSpec((None, d1, tm // d1, DIL_W), lambda bi, i: (bi, 0, i, 0)),
                   pl.BlockSpec((None, d2, tm // d2, DIL_W), lambda bi, i: (bi, 0, i, 0))],
        scratch_shapes=[pltpu.VMEM((2 * DIL_W // LANES, tm, LANES), F32)],
        compiler_params=_params("parallel", "parallel"),
        name="inproj_dil",
    )(x, g, w)


def _natten_bias(rpb):
    ri = np.arange(NA_Q_ROWS)[:, None]
    kr = np.arange(NA_K_ROWS)[None, :]
    c = np.arange(GRID_W)[:, None]
    kc = np.arange(GRID_W)[None, :]
    qstart = np.clip(c - NA_WIN_COLS // 2, 0, GRID_W - NA_WIN_COLS)
    col_ok = (kc >= qstart) & (kc < qstart + NA_WIN_COLS)
    col_idx = np.clip(kc - c + NA_WIN_COLS - 1, 0, 2 * NA_WIN_COLS - 2)
    col_sel = (col_idx[:, :, None] == np.arange(2 * NA_WIN_COLS - 1)) & col_ok[:, :, None]
    row_sel, row_ok = [], []
    for variant, q_minus_k_row0 in enumerate((0, NA_WIN_ROWS // 2, NA_WIN_ROWS)):
        first = (0 * ri, ri, 0 * ri + NA_K_ROWS - NA_WIN_ROWS)[variant]
        ok = (kr >= first) & (kr < first + NA_WIN_ROWS)
        idx = np.clip(kr - q_minus_k_row0 - ri + NA_WIN_ROWS - 1, 0, 2 * NA_WIN_ROWS - 2)
        row_sel.append((idx[:, :, None] == np.arange(2 * NA_WIN_ROWS - 1)) & ok[:, :, None])
        row_ok.append(ok)
    nrow = 2 * NA_WIN_ROWS - 1
    row_sel2 = np.stack(row_sel).reshape(3, NA_Q_ROWS, NA_K_ROWS // 2, 2, nrow)
    col_sel2 = np.zeros((GRID_W, 2, GRID_W, 2, 2 * NA_WIN_COLS - 1), bool)
    for e in range(2):
        col_sel2[:, e, :, e, :] = col_sel
    col_sel2 = col_sel2.reshape(GRID_W, LANES, 2, 2 * NA_WIN_COLS - 1)
    vals = jnp.einsum("vrkea,ham,cjem->vhrckj", jnp.asarray(row_sel2, F32), rpb.astype(F32),
                      jnp.asarray(col_sel2, F32), precision=lax.Precision.HIGHEST)
    ok = np.stack(row_ok)[:, None, :, None, :, None] & col_ok[None, None, None, :, None, :]
    ok = ok.reshape(3, 1, NA_Q_ROWS, GRID_W, NA_K_ROWS // 2, LANES)
    vals = jnp.where(jnp.asarray(ok), vals * LOG2E, NEG_INF)
    return vals.reshape(3, NA_HEADS, NA_TQ, NA_TK)


def _softmax_pv_phases(score_fns, value_fns, s_ref):
    heads = range(len(score_fns))
    lane_max = []
    for h in heads:
        s = score_fns[h]()
        s_ref[h] = s
        lane_max.append(functools.reduce(jnp.maximum, _lane_tiles(s)))
    row_max = [jnp.max(x, axis=-1, keepdims=True) for x in lane_max]
    pv, lane_sum = [], []
    for h in heads:
        e = jnp.exp2(s_ref[h] - row_max[h])
        lane_sum.append(functools.reduce(jnp.add, _lane_tiles(e)))
        pv.append(jnp.dot(e.astype(BF16), value_fns[h](), preferred_element_type=F32))
    row_sum = [jnp.sum(x, axis=-1, keepdims=True) for x in lane_sum]
    return pv, row_max, row_sum


def _natten_kernel(bias_ref, q_ref, k_ref, v_ref, z_ref, o_ref, s_ref, *, rows):
    i = pl.program_id(1)
    krow0 = jnp.clip(i * NA_Q_ROWS - NA_WIN_ROWS // 2, 0, rows - NA_K_ROWS)
    kstart = pl.multiple_of(krow0 * GRID_W, GRID_W)
    masks = _head_masks(1.0)
    lane = lax.broadcasted_iota(jnp.int32, (NA_TQ, LANES), 1)

    def score_fn(h):
        cols = slice(h // 2 * LANES, (h // 2 + 1) * LANES)
        return lambda: (lax.dot_general(q_ref[:, cols] * masks[h % 2], k_ref[pl.ds(kstart, NA_TK), cols], _NT,
                                        preferred_element_type=F32) + bias_ref[h])

    def value_fn(h):
        cols = slice(h // 2 * LANES, (h // 2 + 1) * LANES)
        return lambda: v_ref[pl.ds(kstart, NA_TK), cols]

    pv, _, row_sum = _softmax_pv_phases([score_fn(h) for h in range(NA_HEADS)],
                                        [value_fn(h) for h in range(NA_HEADS)], s_ref)
    for p in range(NA_HEADS // 2):
        cols = slice(p * LANES, (p + 1) * LANES)
        y = jnp.where(lane < HEAD_DIM, pv[2 * p] * (1.0 / row_sum[2 * p]), pv[2 * p + 1] * (1.0 / row_sum[2 * p + 1]))
        o_ref[:, cols] = (y * _silu(z_ref[:, cols].astype(F32))).astype(BF16)


def _natten(u, bias, b, l):
    rows = l // GRID_W
    nblk = l // NA_TQ
    assert rows >= NA_K_ROWS and nblk >= 3

    def bias_map(bi, i):
        return (jnp.where(i == 0, 0, jnp.where(i == nblk - 1, 2, 1)), 0, 0, 0)

    return pl.pallas_call(
        functools.partial(_natten_kernel, rows=rows),
        out_shape=jax.ShapeDtypeStruct((b * l, A_W), BF16),
        grid=(b, nblk),
        in_specs=[
            pl.BlockSpec((None, NA_HEADS, NA_TQ, NA_TK), bias_map),
            pl.BlockSpec((NA_TQ, A_W), lambda bi, i: (bi * nblk + i, OFF_QA // A_W)),
            pl.BlockSpec((l, A_W), lambda bi, i: (bi, OFF_KA // A_W)),
            pl.BlockSpec((l, A_W), lambda bi, i: (bi, OFF_VA // A_W)),
            pl.BlockSpec((NA_TQ, A_W), lambda bi, i: (bi * nblk + i, OFF_ZA // A_W)),
        ],
        out_specs=pl.BlockSpec((NA_TQ, A_W), lambda bi, i: (bi * nblk + i, 0)),
        scratch_shapes=[pltpu.VMEM((NA_HEADS, NA_TQ, NA_TK), F32)],
        compiler_params=_params("parallel", "arbitrary"),
        name="natten",
    )(bias, u, u, u, u)


def _diff_slopes():
    return [float(np.float32(2.0 ** (-8.0 * (i + 1) / DIFF_HEADS))) for i in range(DIFF_HEADS)]


def _diff_bias_table(l, tq, tk):
    nvar = (l - tk) // tq + (l - tq) // tq + 1
    n0 = (l - tk) // tq
    delta = (jnp.arange(nvar, dtype=jnp.int32) - n0) * tq
    q_minus_k = jnp.arange(tq, dtype=jnp.int32)[:, None] - jnp.arange(tk, dtype=jnp.int32)[None, :]
    dist = jnp.abs(q_minus_k[None] + delta[:, None, None]).astype(F32)
    slopes = jnp.asarray(_diff_slopes(), F32) * LOG2E
    return -slopes[:, None, None, None] * dist[None], n0


def _diff_kernel(lq_ref, gd_ref, bias_ref, q_ref, k_ref, v_ref, z_ref, o_ref, s_ref, st_ref, *,
                 l, tq, tk, nblk, n0, lam_init):
    i = pl.program_id(2)
    nck = l // tk
    ntile = tk // LANES
    lq = lq_ref[...]
    lam = (jnp.exp(jnp.sum(lq[0:1] * lq[1:2], axis=-1, keepdims=True))
           - jnp.exp(jnp.sum(lq[2:3] * lq[3:4], axis=-1, keepdims=True)) + lam_init)
    masks = _head_masks(1.0)
    blocks = range(nblk)

    for blk in blocks:
        q = q_ref[blk * tq:(blk + 1) * tq, :]
        for c in range(nck):
            k = k_ref[c * tk:(c + 1) * tk, :]
            bias = bias_ref[i * nblk + blk + (n0 - c * (tk // tq))]
            for n in range(2):
                s = lax.dot_general(q * masks[n], k, _NT, preferred_element_type=F32) + bias
                tiles = _lane_tiles(s)
                mloc = functools.reduce(jnp.maximum, tiles)
                es = [jnp.exp2(t - mloc) for t in tiles]
                for t in range(ntile):
                    s_ref[blk, n, c, :, t * LANES:(t + 1) * LANES] = es[t]
                st_ref[blk, 0, n, c] = mloc
                st_ref[blk, 1, n, c] = functools.reduce(jnp.add, es)

    for blk in blocks:
        coef = []
        for n in range(2):
            m = jnp.max(functools.reduce(jnp.maximum, [st_ref[blk, 0, n, c] for c in range(nck)]),
                        axis=-1, keepdims=True)
            fs = [jnp.exp2(st_ref[blk, 0, n, c] - m) for c in range(nck)]
            den = jnp.sum(functools.reduce(jnp.add, [fs[c] * st_ref[blk, 1, n, c] for c in range(nck)]),
                          axis=-1, keepdims=True)
            scale = (1.0 / den) if n == 0 else (lam / den)
            coef.append([f * scale for f in fs])

        o = None
        for c in range(nck):
            w = jnp.concatenate(
                [s_ref[blk, 0, c, :, t * LANES:(t + 1) * LANES] * coef[0][c]
                 - s_ref[blk, 1, c, :, t * LANES:(t + 1) * LANES] * coef[1][c] for t in range(ntile)], axis=-1)
            pv = jnp.dot(w.astype(BF16), v_ref[c * tk:(c + 1) * tk, :], preferred_element_type=F32)
            o = pv if o is None else o + pv

        y = (_rms_scale(o) * gd_ref[...]) * (1.0 - lam_init)
        rows = slice(blk * tq, (blk + 1) * tq)
        o_ref[rows, :] = (y * _silu(z_ref[rows, :].astype(F32))).astype(BF16)


def _diff(u, lam_qk, g_diff, b, l, lam_init):
    tq = min(DIFF_TQ, l)
    tk = min(DIFF_TK, l)
    nblk = min(DIFF_BLOCKS, l // tq)
    nq = l // (nblk * tq)
    qb, kb, vb, zb = (off // LANES for off in (OFF_QB, OFF_KB, OFF_VB, OFF_ZB))
    bias, n0 = _diff_bias_table(l, tq, tk)
    nvar = bias.shape[1]
    return pl.pallas_call(
        functools.partial(_diff_kernel, l=l, tq=tq, tk=tk, nblk=nblk, n0=n0, lam_init=lam_init),
        out_shape=jax.ShapeDtypeStruct((b * l, B_W), BF16),
        grid=(DIFF_HEADS, b, nq),
        in_specs=[
            pl.BlockSpec((4, HEAD_DIM), lambda h, bi, i: (0, 0)),
            pl.BlockSpec((1, 2 * HEAD_DIM), lambda h, bi, i: (0, 0)),
            pl.BlockSpec((None, nvar, tq, tk), lambda h, bi, i: (h, 0, 0, 0), pipeline_mode=pl.Buffered(1)),
            pl.BlockSpec((nblk * tq, LANES), lambda h, bi, i: (bi * nq + i, qb + h)),
            pl.BlockSpec((l, LANES), lambda h, bi, i: (bi, kb + h)),
            pl.BlockSpec((l, LANES), lambda h, bi, i: (bi, vb + h)),
            pl.BlockSpec((nblk * tq, LANES), lambda h, bi, i: (bi * nq + i, zb + h)),
        ],
        out_specs=pl.BlockSpec((nblk * tq, LANES), lambda h, bi, i: (bi * nq + i, h)),
        scratch_shapes=[pltpu.VMEM((nblk, 2, l // tk, tq, tk), F32),
                        pltpu.VMEM((nblk, 2, 2, l // tk, tq, LANES), F32)],
        compiler_params=_params("parallel", "parallel", "arbitrary"),
        name="diffattn",
    )(lam_qk, g_diff, bias, u, u, u, u)


def _dil_slopes():
    n = DIL_GROUPS * DIL_HEADS
    return [float(np.float32(2.0 ** (-8.0 * (i + 1) / n))) for i in range(n)]


def _dilated_kernel(q_ref, k_ref, v_ref, o_ref, lse_ref, s_ref, *, m, tq, tw, nclass, nsub, slopes):
    j = pl.program_id(2)
    masks = _head_masks(1.0)
    lane = lax.broadcasted_iota(jnp.int32, (tq, LANES), 1)
    units = [(r, sb) for r in range(nclass) for sb in range(nsub)]
    kstarts, dists = [], []
    for sb in range(nsub):
        q0pos = (j * nsub + sb) * tq
        kstart = pl.multiple_of(jnp.clip(q0pos - DIL_SIDE, 0, m - tw), DIL_SIDE)
        rel = (kstart + lax.broadcasted_iota(jnp.int32, (tq, tw), 1)
               - q0pos - lax.broadcasted_iota(jnp.int32, (tq, tw), 0))
        kstarts.append(kstart)
        dists.append(jnp.abs(rel).astype(F32))

    def score_fn(r, sb, h):
        cols = slice(h // 2 * LANES, (h // 2 + 1) * LANES)

        def fn():
            q = q_ref[r, sb * tq:(sb + 1) * tq, cols] * masks[h % 2]
            s = lax.dot_general(q, k_ref[r, pl.ds(kstarts[sb], tw), cols], _NT, preferred_element_type=F32)
            return jnp.where(dists[sb] <= float(DIL_SIDE), s - slopes[h] * dists[sb], NEG_INF)
        return fn

    def value_fn(r, sb, h):
        cols = slice(h // 2 * LANES, (h // 2 + 1) * LANES)
        return lambda: v_ref[r, pl.ds(kstarts[sb], tw), cols]

    heads = [(r, sb, h) for r, sb in units for h in range(DIL_HEADS)]
    pv, row_max, row_sum = _softmax_pv_phases([score_fn(*x) for x in heads], [value_fn(*x) for x in heads], s_ref)
    for n, (r, sb) in enumerate(units):
        for p in range(DIL_HEADS // 2):
            a, b = n * DIL_HEADS + 2 * p, n * DIL_HEADS + 2 * p + 1
            rows, cols = slice(sb * tq, (sb + 1) * tq), slice(p * LANES, (p + 1) * LANES)
            o_ref[r, rows, cols] = jnp.where(lane < HEAD_DIM, pv[a] * (1.0 / row_sum[a]), pv[b] * (1.0 / row_sum[b]))
            lse_ref[r, rows, cols] = jnp.where(lane < HEAD_DIM, row_max[a] + jnp.log2(row_sum[a]),
                                               row_max[b] + jnp.log2(row_sum[b]))


def _dilated_group(qkv, col0, b, l, g):
    dil = DIL_PAIRS[g][1]
    assert DIL_PAIRS[g][0] // (2 * dil) == DIL_SIDE
    m = l // dil
    tq = min(DIL_TQ, m)
    tw = min(tq + 2 * DIL_SIDE, m)
    nsub = min(DIL_UNITS, m // tq)
    nclass = min(DIL_UNITS // nsub, dil)
    slopes = tuple(s * dil * LOG2E for s in _dil_slopes()[g * DIL_HEADS:(g + 1) * DIL_HEADS])
    return pl.pallas_call(
        functools.partial(_dilated_kernel, m=m, tq=tq, tw=tw, nclass=nclass, nsub=nsub, slopes=slopes),
        out_shape=[jax.ShapeDtypeStruct((b, dil, m, C_W), F32)] * 2,
        grid=(b, dil // nclass, m // (nsub * tq)),
        in_specs=[
            pl.BlockSpec((None, nclass, nsub * tq, C_W), lambda bi, r, j: (bi, r, j, col0)),
            pl.BlockSpec((None, nclass, m, C_W), lambda bi, r, j: (bi, r, 0, col0 + 1)),
            pl.BlockSpec((None, nclass, m, C_W), lambda bi, r, j: (bi, r, 0, col0 + 2)),
        ],
        out_specs=[pl.BlockSpec((None, nclass, nsub * tq, C_W), lambda bi, r, j: (bi, r, j, 0))] * 2,
        scratch_shapes=[pltpu.VMEM((nclass * nsub * DIL_HEADS, tq, tw), F32)],
        compiler_params=_params("parallel", "parallel", "arbitrary"),
        name=f"dilated{g}",
    )(qkv, qkv, qkv)


def _merge_kernel(x_ref, ga_ref, gb_ref, o0_ref, l0_ref, o1_ref, l1_ref, o2_ref, l2_ref, zc_ref, gl_ref,
                  bg_ref, wa_ref, wb_ref, wc_ref, wo_ref, gf_ref, out_ref, *scratch, tm, final_norm):
    outs, lses = [o0_ref[0]], [l0_ref[0]]
    for g, refs in ((1, (o1_ref, l1_ref)), (2, (o2_ref, l2_ref))):
        dil = DIL_PAIRS[g][1]
        for src, dst in zip(refs, scratch[2 * (g - 1):2 * g]):
            for t in range(C_W // LANES):
                for r in range(dil):
                    dst[t, pl.ds(r, tm // dil, stride=dil), :] = src[r, :, t * LANES:(t + 1) * LANES]
        outs.append(jnp.concatenate([scratch[2 * (g - 1)][t] for t in range(C_W // LANES)], axis=-1))
        lses.append(jnp.concatenate([scratch[2 * (g - 1) + 1][t] for t in range(C_W // LANES)], axis=-1))
    mx = jnp.maximum(jnp.maximum(lses[0], lses[1]), lses[2])
    wts = [jnp.exp2(x - mx) for x in lses]
    den = wts[0] + wts[1] + wts[2]
    yc = (wts[0] * outs[0] + wts[1] * outs[1] + wts[2] * outs[2]) / den
    gc = (yc * _silu(zc_ref[...].astype(F32))).astype(BF16)
    branches = (jnp.dot(ga_ref[...], wa_ref[...], preferred_element_type=F32),
                jnp.dot(gb_ref[...], wb_ref[...], preferred_element_type=F32),
                jnp.dot(gc, wc_ref[...], preferred_element_type=F32))
    merged = None
    for n, pn in enumerate(branches):
        logit = gl_ref[:, n * D_MODEL:(n + 1) * D_MODEL].astype(F32) + bg_ref[n:n + 1, :]
        term = pn / (1.0 + jnp.exp(-logit))
        merged = term if merged is None else merged + term
    y = x_ref[...] + jnp.dot(merged.astype(BF16), wo_ref[...], preferred_element_type=F32)
    if final_norm:
        y = _rms_scale(y) * gf_ref[...]
    out_ref[...] = y


def _merge(x, ga, gb, dil, u, b_gate, wa, wb, wc, wo, g_final, b, l, final_norm):
    tm = min(MERGE_TM, l)
    nt = l // tm
    row = lambda w, col=0: pl.BlockSpec((tm, w), lambda bi, i: (bi * nt + i, col))
    full = lambda a: pl.BlockSpec(a.shape, lambda bi, i: (0, 0))
    cm = lambda g: pl.BlockSpec((None, DIL_PAIRS[g][1], tm // DIL_PAIRS[g][1], C_W), lambda bi, i: (bi, 0, i, 0))
    dil_specs = [cm(g) for g in range(DIL_GROUPS) for _ in range(2)]
    dil_args = [a for pair in dil for a in pair]
    return pl.pallas_call(
        functools.partial(_merge_kernel, tm=tm, final_norm=final_norm),
        out_shape=jax.ShapeDtypeStruct((b * l, D_MODEL), F32),
        grid=(b, nt),
        in_specs=[row(D_MODEL), row(A_W), row(B_W)] + dil_specs
                 + [row(C_W, OFF_ZC // C_W), row(GL_W, OFF_GL // GL_W)]
                 + [full(a) for a in (b_gate, wa, wb, wc, wo, g_final)],
        out_specs=row(D_MODEL),
        scratch_shapes=[pltpu.VMEM((C_W // LANES, tm, LANES), F32)] * 4,
        compiler_params=_params("parallel", "parallel"),
        name="merge",
    )(x, ga, gb, *dil_args, u, u, b_gate, wa, wb, wc, wo, g_final)


def _trunk(x, layers, g_final):
    b, l, _ = x.shape
    x = x.reshape(b * l, D_MODEL)
    for n, p in enumerate(layers):
        lam_init = 0.8 - 0.6 * math.exp(-0.3 * n)
        u = _inproj(x, p["g_norm"], p["w_main"])
        c1, c2 = _inproj_dil(x, p["g_norm"], p["w_dil"], b, l)
        ga = _natten(u, p["na_bias"], b, l)
        gb = _diff(u, p["lam_qk"], p["g_diff"], b, l, lam_init)
        dil = [_dilated_group(u.reshape(b, 1, l, MAIN_W), OFF_C0 // C_W, b, l, 0),
               _dilated_group(c1, 0, b, l, 1),
               _dilated_group(c2, 0, b, l, 2)]
        x = _merge(x, ga, gb, dil, u, p["b_gate"], p["w_br_a"], p["w_br_b"], p["w_br_c"], p["w_out"],
                   g_final, b, l, final_norm=(n == len(layers) - 1))
    return x.reshape(b, l, D_MODEL)


def _prepare_layers(g_norm, w_in, b_gate, rpb, lam_qk, g_diff, w_br_a, w_br_b, w_br_c, w_out):
    segments = _column_segments()
    layers = []
    for n in range(w_in.shape[0]):
        w = jnp.concatenate([w_in[n][:, s:s + width] * scale for s, width, scale in segments], axis=1).astype(BF16)
        layers.append(dict(
            g_norm=g_norm[n].reshape(1, D_MODEL).astype(F32),
            w_main=w[:, :MAIN_W],
            w_dil=w[:, MAIN_W:],
            b_gate=b_gate[n].astype(F32),
            na_bias=_natten_bias(rpb[n]),
            lam_qk=lam_qk[n].astype(F32),
            g_diff=g_diff[n].reshape(1, 2 * HEAD_DIM).astype(F32),
            w_br_a=w_br_a[n].astype(BF16),
            w_br_b=w_br_b[n].astype(BF16),
            w_br_c=w_br_c[n].astype(BF16),
            w_out=w_out[n].astype(BF16),
        ))
    return layers


def kernel(x_prompt, x_sample, g_norm, w_in, b_gate, rpb, lam_qk, g_diff, w_br_a, w_br_b, w_br_c, w_out, g_final):
    layers = _prepare_layers(g_norm, w_in, b_gate, rpb, lam_qk, g_diff, w_br_a, w_br_b, w_br_c, w_out)
    gf = g_final.reshape(1, D_MODEL).astype(F32)
    return (_trunk(x_prompt, layers, gf), _trunk(x_sample, layers, gf))
```

```python
import functools
import math

import numpy as np
import jax
import jax.numpy as jnp
from jax import lax
from jax.experimental import pallas as pl
from jax.experimental.pallas import tpu as pltpu

F32 = jnp.float32
BF16 = jnp.bfloat16

D_MODEL = 1024
GRID_W = 64
HEAD_DIM = 64
RMS_EPS = 1e-6
NEG_INF = -1e30
LANES = 128

NA_HEADS = 8
NA_WIN_ROWS = 8
NA_WIN_COLS = 16
NA_Q_ROWS = 4
NA_K_ROWS = NA_Q_ROWS + NA_WIN_ROWS
NA_TQ = NA_Q_ROWS * GRID_W
NA_TK = NA_K_ROWS * GRID_W

DIFF_HEADS = 4
DIFF_TQ = 256
DIFF_TK = 512
DIFF_BLOCKS = 2

DIL_PAIRS = ((128, 1), (512, 4), (2048, 16))
DIL_GROUPS = 3
DIL_HEADS = 4
DIL_SIDE = 64
DIL_TQ = 256
DIL_UNITS = 4

A_W = NA_HEADS * HEAD_DIM
B_W = DIFF_HEADS * 2 * HEAD_DIM
C_QKV = DIL_GROUPS * DIL_HEADS * HEAD_DIM
C_W = DIL_HEADS * HEAD_DIM
N_BRANCH = 3
GL_W = N_BRANCH * D_MODEL
IN_WIDTH = 4 * A_W + 4 * B_W + 3 * C_QKV + C_W + GL_W

OFF_GL = 0
OFF_QA = GL_W
OFF_KA = OFF_QA + A_W
OFF_VA = OFF_KA + A_W
OFF_ZA = OFF_VA + A_W
OFF_QB = OFF_ZA + A_W
OFF_KB = OFF_QB + B_W
OFF_VB = OFF_KB + B_W
OFF_ZB = OFF_VB + B_W
OFF_ZC = OFF_ZB + B_W
OFF_C0 = OFF_ZC + C_W
MAIN_W = OFF_C0 + 3 * C_W
DIL_W = 3 * C_W
assert MAIN_W + 2 * DIL_W == IN_WIDTH

VMEM_LIMIT = 56 * 1024 * 1024

IN_TM = 1024
IN_TN = MAIN_W // 2
MERGE_TM = 512

_NT = (((1,), (1,)), ((), ()))
LOG2E = math.log2(math.e)
Q_SCALE = HEAD_DIM ** -0.5 * LOG2E


def _column_segments():
    b0 = 4 * A_W
    c0 = b0 + 4 * B_W
    zc0 = c0 + 3 * C_QKV
    qkv = lambda g: [(c0 + n * C_QKV + g * C_W, C_W, Q_SCALE if n == 0 else 1.0) for n in range(3)]
    return ([(zc0 + C_W, GL_W, 1.0), (0, A_W, Q_SCALE), (A_W, 3 * A_W, 1.0), (b0, B_W, Q_SCALE),
             (b0 + B_W, 3 * B_W, 1.0), (zc0, C_W, 1.0)] + qkv(0) + qkv(1) + qkv(2))


def _params(*sem):
    return pltpu.CompilerParams(dimension_semantics=sem, vmem_limit_bytes=VMEM_LIMIT)


def _silu(z):
    return z / (1.0 + jnp.exp(-z))


def _rms_scale(x):
    return x * lax.rsqrt(jnp.mean(x * x, axis=-1, keepdims=True) + RMS_EPS)


def _head_masks(scale):
    lane = lax.broadcasted_iota(jnp.int32, (1, LANES), 1)
    lo = jnp.where(lane < HEAD_DIM, scale, 0.0).astype(BF16)
    hi = jnp.where(lane >= HEAD_DIM, scale, 0.0).astype(BF16)
    return lo, hi


def _lane_tiles(x):
    if x.shape[-1] <= LANES:
        return [x]
    return [x[:, n * LANES:(n + 1) * LANES] for n in range(x.shape[-1] // LANES)]


def _inproj_kernel(x_ref, g_ref, w_ref, o_ref, h_ref):
    @pl.when(pl.program_id(1) == 0)
    def _():
        h_ref[...] = (_rms_scale(x_ref[...]) * g_ref[...]).astype(BF16)

    o_ref[...] = jnp.dot(h_ref[...], w_ref[...], preferred_element_type=F32).astype(BF16)


def _inproj(x, g, w):
    t = x.shape[0]
    tm = min(IN_TM, t)
    return pl.pallas_call(
        _inproj_kernel,
        out_shape=jax.ShapeDtypeStruct((t, MAIN_W), BF16),
        grid=(t // tm, MAIN_W // IN_TN),
        in_specs=[
            pl.BlockSpec((tm, D_MODEL), lambda i, j: (i, 0)),
            pl.BlockSpec((1, D_MODEL), lambda i, j: (0, 0)),
            pl.BlockSpec((D_MODEL, IN_TN), lambda i, j: (0, j)),
        ],
        out_specs=pl.BlockSpec((tm, IN_TN), lambda i, j: (i, j)),
        scratch_shapes=[pltpu.VMEM((tm, D_MODEL), BF16)],
        compiler_params=_params("parallel", "arbitrary"),
        name="inproj",
    )(x, g, w)


def _inproj_dil_kernel(x_ref, g_ref, w_ref, o1_ref, o2_ref, r_ref, *, tm):
    h = (_rms_scale(x_ref[...]) * g_ref[...]).astype(BF16)
    res = jnp.dot(h, w_ref[...], preferred_element_type=F32)
    for t, tile in enumerate(_lane_tiles(res)):
        r_ref[t] = tile
    tiles_per_group = DIL_W // LANES
    for g, o_ref in ((1, o1_ref), (2, o2_ref)):
        dil = DIL_PAIRS[g][1]
        for t in range(tiles_per_group):
            for r in range(dil):
                rows = r_ref[(g - 1) * tiles_per_group + t, pl.ds(r, tm // dil, stride=dil), :]
                o_ref[r, :, t * LANES:(t + 1) * LANES] = rows.astype(BF16)


def _inproj_dil(x, g, w, b, l):
    tm = min(IN_TM, l)
    nt = l // tm
    d1, d2 = DIL_PAIRS[1][1], DIL_PAIRS[2][1]
    return pl.pallas_call(
        functools.partial(_inproj_dil_kernel, tm=tm),
        out_shape=[jax.ShapeDtypeStruct((b, d1, l // d1, DIL_W), BF16),
                   jax.ShapeDtypeStruct((b, d2, l // d2, DIL_W), BF16)],
        grid=(b, nt),
        in_specs=[
            pl.BlockSpec((tm, D_MODEL), lambda bi, i: (bi * nt + i, 0)),
            pl.BlockSpec((1, D_MODEL), lambda bi, i: (0, 0)),
            pl.BlockSpec((D_MODEL, 2 * DIL_W), lambda bi, i: (0, 0)),
        ],
        out_specs=[pl.BlockSpec((None, d1, tm // d1, DIL_W), lambda bi, i: (bi, 0, i, 0)),
                   pl.BlockSpec((None, d2, tm // d2, DIL_W), lambda bi, i: (bi, 0, i, 0))],
        scratch_shapes=[pltpu.VMEM((2 * DIL_W // LANES, tm, LANES), F32)],
        compiler_params=_params("parallel", "parallel"),
        name="inproj_dil",
    )(x, g, w)


def _natten_bias(rpb):
    ri = np.arange(NA_Q_ROWS)[:, None]
    kr = np.arange(NA_K_ROWS)[None, :]
    c = np.arange(GRID_W)[:, None]
    kc = np.arange(GRID_W)[None, :]
    qstart = np.clip(c - NA_WIN_COLS // 2, 0, GRID_W - NA_WIN_COLS)
    col_ok = (kc >= qstart) & (kc < qstart + NA_WIN_COLS)
    col_idx = np.clip(kc - c + NA_WIN_COLS - 1, 0, 2 * NA_WIN_COLS - 2)
    col_sel = (col_idx[:, :, None] == np.arange(2 * NA_WIN_COLS - 1)) & col_ok[:, :, None]
    row_sel, row_ok = [], []
    for variant, q_minus_k_row0 in enumerate((0, NA_WIN_ROWS // 2, NA_WIN_ROWS)):
        first = (0 * ri, ri, 0 * ri + NA_K_ROWS - NA_WIN_ROWS)[variant]
        ok = (kr >= first) & (kr < first + NA_WIN_ROWS)
        idx = np.clip(kr - q_minus_k_row0 - ri + NA_WIN_ROWS - 1, 0, 2 * NA_WIN_ROWS - 2)
        row_sel.append((idx[:, :, None] == np.arange(2 * NA_WIN_ROWS - 1)) & ok[:, :, None])
        row_ok.append(ok)
    nrow = 2 * NA_WIN_ROWS - 1
    row_sel2 = np.stack(row_sel).reshape(3, NA_Q_ROWS, NA_K_ROWS // 2, 2, nrow)
    col_sel2 = np.zeros((GRID_W, 2, GRID_W, 2, 2 * NA_WIN_COLS - 1), bool)
    for e in range(2):
        col_sel2[:, e, :, e, :] = col_sel
    col_sel2 = col_sel2.reshape(GRID_W, LANES, 2, 2 * NA_WIN_COLS - 1)
    vals = jnp.einsum("vrkea,ham,cjem->vhrckj", jnp.asarray(row_sel2, F32), rpb.astype(F32),
                      jnp.asarray(col_sel2, F32), precision=lax.Precision.HIGHEST)
    ok = np.stack(row_ok)[:, None, :, None, :, None] & col_ok[None, None, None, :, None, :]
    ok = ok.reshape(3, 1, NA_Q_ROWS, GRID_W, NA_K_ROWS // 2, LANES)
    vals = jnp.where(jnp.asarray(ok), vals * LOG2E, NEG_INF)
    return vals.reshape(3, NA_HEADS, NA_TQ, NA_TK)


def _softmax_pv_phases(score_fns, value_fns, s_ref):
    heads = range(len(score_fns))
    lane_max = []
    for h in heads:
        s = score_fns[h]()
        s_ref[h] = s
        lane_max.append(functools.reduce(jnp.maximum, _lane_tiles(s)))
    row_max = [jnp.max(x, axis=-1, keepdims=True) for x in lane_max]
    pv, lane_sum = [], []
    for h in heads:
        e = jnp.exp2(s_ref[h] - row_max[h])
        lane_sum.append(functools.reduce(jnp.add, _lane_tiles(e)))
        pv.append(jnp.dot(e.astype(BF16), value_fns[h](), preferred_element_type=F32))
    row_sum = [jnp.sum(x, axis=-1, keepdims=True) for x in lane_sum]
    return pv, row_max, row_sum


def _natten_kernel(bias_ref, q_ref, k_ref, v_ref, z_ref, o_ref, s_ref, *, rows):
    i = pl.program_id(1)
    krow0 = jnp.clip(i * NA_Q_ROWS - NA_WIN_ROWS // 2, 0, rows - NA_K_ROWS)
    kstart = pl.multiple_of(krow0 * GRID_W, GRID_W)
    masks = _head_masks(1.0)
    lane = lax.broadcasted_iota(jnp.int32, (NA_TQ, LANES), 1)

    def score_fn(h):
        cols = slice(h // 2 * LANES, (h // 2 + 1) * LANES)
        return lambda: (lax.dot_general(q_ref[:, cols] * masks[h % 2], k_ref[pl.ds(kstart, NA_TK), cols], _NT,
                                        preferred_element_type=F32) + bias_ref[h])

    def value_fn(h):
        cols = slice(h // 2 * LANES, (h // 2 + 1) * LANES)
        return lambda: v_ref[pl.ds(kstart, NA_TK), cols]

    pv, _, row_sum = _softmax_pv_phases([score_fn(h) for h in range(NA_HEADS)],
                                        [value_fn(h) for h in range(NA_HEADS)], s_ref)
    for p in range(NA_HEADS // 2):
        cols = slice(p * LANES, (p + 1) * LANES)
        y = jnp.where(lane < HEAD_DIM, pv[2 * p] * (1.0 / row_sum[2 * p]), pv[2 * p + 1] * (1.0 / row_sum[2 * p + 1]))
        o_ref[:, cols] = (y * _silu(z_ref[:, cols].astype(F32))).astype(BF16)


def _natten(u, bias, b, l):
    rows = l // GRID_W
    nblk = l // NA_TQ
    assert rows >= NA_K_ROWS and nblk >= 3

    def bias_map(bi, i):
        return (jnp.where(i == 0, 0, jnp.where(i == nblk - 1, 2, 1)), 0, 0, 0)

    return pl.pallas_call(
        functools.partial(_natten_kernel, rows=rows),
        out_shape=jax.ShapeDtypeStruct((b * l, A_W), BF16),
        grid=(b, nblk),
        in_specs=[
            pl.BlockSpec((None, NA_HEADS, NA_TQ, NA_TK), bias_map),
            pl.BlockSpec((NA_TQ, A_W), lambda bi, i: (bi * nblk + i, OFF_QA // A_W)),
            pl.BlockSpec((l, A_W), lambda bi, i: (bi, OFF_KA // A_W)),
            pl.BlockSpec((l, A_W), lambda bi, i: (bi, OFF_VA // A_W)),
            pl.BlockSpec((NA_TQ, A_W), lambda bi, i: (bi * nblk + i, OFF_ZA // A_W)),
        ],
        out_specs=pl.BlockSpec((NA_TQ, A_W), lambda bi, i: (bi * nblk + i, 0)),
        scratch_shapes=[pltpu.VMEM((NA_HEADS, NA_TQ, NA_TK), F32)],
        compiler_params=_params("parallel", "arbitrary"),
        name="natten",
    )(bias, u, u, u, u)


def _diff_slopes():
    return [float(np.float32(2.0 ** (-8.0 * (i + 1) / DIFF_HEADS))) for i in range(DIFF_HEADS)]


def _diff_bias_table(l, tq, tk):
    nvar = (l - tk) // tq + (l - tq) // tq + 1
    n0 = (l - tk) // tq
    delta = (jnp.arange(nvar, dtype=jnp.int32) - n0) * tq
    q_minus_k = jnp.arange(tq, dtype=jnp.int32)[:, None] - jnp.arange(tk, dtype=jnp.int32)[None, :]
    dist = jnp.abs(q_minus_k[None] + delta[:, None, None]).astype(F32)
    slopes = jnp.asarray(_diff_slopes(), F32) * LOG2E
    return -slopes[:, None, None, None] * dist[None], n0


def _diff_kernel(lq_ref, gd_ref, bias_ref, q_ref, k_ref, v_ref, z_ref, o_ref, s_ref, st_ref, *,
                 l, tq, tk, nblk, n0, lam_init):
    i = pl.program_id(2)
    nck = l // tk
    ntile = tk // LANES
    lq = lq_ref[...]
    lam = (jnp.exp(jnp.sum(lq[0:1] * lq[1:2], axis=-1, keepdims=True))
           - jnp.exp(jnp.sum(lq[2:3] * lq[3:4], axis=-1, keepdims=True)) + lam_init)
    masks = _head_masks(1.0)
    blocks = range(nblk)

    for blk in blocks:
        q = q_ref[blk * tq:(blk + 1) * tq, :]
        for c in range(nck):
            k = k_ref[c * tk:(c + 1) * tk, :]
            bias = bias_ref[i * nblk + blk + (n0 - c * (tk // tq))]
            for n in range(2):
                s = lax.dot_general(q * masks[n], k, _NT, preferred_element_type=F32) + bias
                tiles = _lane_tiles(s)
                mloc = functools.reduce(jnp.maximum, tiles)
                es = [jnp.exp2((t - mloc).astype(BF16)) for t in tiles]
                for t in range(ntile):
                    s_ref[blk, n, c, :, t * LANES:(t + 1) * LANES] = es[t]
                st_ref[blk, 0, n, c] = mloc
                st_ref[blk, 1, n, c] = functools.reduce(jnp.add, es).astype(F32)

    for blk in blocks:
        coef = []
        for n in range(2):
            m = jnp.max(functools.reduce(jnp.maximum, [st_ref[blk, 0, n, c] for c in range(nck)]),
                        axis=-1, keepdims=True)
            fs = [jnp.exp2(st_ref[blk, 0, n, c] - m) for c in range(nck)]
            den = jnp.sum(functools.reduce(jnp.add, [fs[c] * st_ref[blk, 1, n, c] for c in range(nck)]),
                          axis=-1, keepdims=True)
            scale = (1.0 / den) if n == 0 else (lam / den)
            coef.append([(f * scale).astype(BF16) for f in fs])

        o = None
        for c in range(nck):
            w = jnp.concatenate(
                [s_ref[blk, 0, c, :, t * LANES:(t + 1) * LANES] * coef[0][c]
                 - s_ref[blk, 1, c, :, t * LANES:(t + 1) * LANES] * coef[1][c] for t in range(ntile)], axis=-1)
            pv = jnp.dot(w, v_ref[c * tk:(c + 1) * tk, :], preferred_element_type=F32)
            o = pv if o is None else o + pv

        y = (_rms_scale(o) * gd_ref[...]) * (1.0 - lam_init)
        rows = slice(blk * tq, (blk + 1) * tq)
        o_ref[rows, :] = (y * _silu(z_ref[rows, :].astype(F32))).astype(BF16)


def _diff(u, lam_qk, g_diff, b, l, lam_init):
    tq = min(DIFF_TQ, l)
    tk = min(DIFF_TK, l)
    nblk = min(DIFF_BLOCKS, l // tq)
    nq = l // (nblk * tq)
    qb, kb, vb, zb = (off // LANES for off in (OFF_QB, OFF_KB, OFF_VB, OFF_ZB))
    bias, n0 = _diff_bias_table(l, tq, tk)
    nvar = bias.shape[1]
    return pl.pallas_call(
        functools.partial(_diff_kernel, l=l, tq=tq, tk=tk, nblk=nblk, n0=n0, lam_init=lam_init),
        out_shape=jax.ShapeDtypeStruct((b * l, B_W), BF16),
        grid=(DIFF_HEADS, b, nq),
        in_specs=[
            pl.BlockSpec((4, HEAD_DIM), lambda h, bi, i: (0, 0)),
            pl.BlockSpec((1, 2 * HEAD_DIM), lambda h, bi, i: (0, 0)),
            pl.BlockSpec((None, nvar, tq, tk), lambda h, bi, i: (h, 0, 0, 0), pipeline_mode=pl.Buffered(1)),
            pl.BlockSpec((nblk * tq, LANES), lambda h, bi, i: (bi * nq + i, qb + h)),
            pl.BlockSpec((l, LANES), lambda h, bi, i: (bi, kb + h)),
            pl.BlockSpec((l, LANES), lambda h, bi, i: (bi, vb + h)),
            pl.BlockSpec((nblk * tq, LANES), lambda h, bi, i: (bi * nq + i, zb + h)),
        ],
        out_specs=pl.BlockSpec((nblk * tq, LANES), lambda h, bi, i: (bi * nq + i, h)),
        scratch_shapes=[pltpu.VMEM((nblk, 2, l // tk, tq, tk), BF16),
                        pltpu.VMEM((nblk, 2, 2, l // tk, tq, LANES), F32)],
        compiler_params=_params("parallel", "parallel", "arbitrary"),
        name="diffattn",
    )(lam_qk, g_diff, bias, u, u, u, u)


def _dil_slopes():
    n = DIL_GROUPS * DIL_HEADS
    return [float(np.float32(2.0 ** (-8.0 * (i + 1) / n))) for i in range(n)]


def _dilated_kernel(q_ref, k_ref, v_ref, o_ref, lse_ref, s_ref, *, m, tq, tw, nclass, nsub, slopes):
    j = pl.program_id(2)
    masks = _head_masks(1.0)
    lane = lax.broadcasted_iota(jnp.int32, (tq, LANES), 1)
    units = [(r, sb) for r in range(nclass) for sb in range(nsub)]
    kstarts, dists = [], []
    for sb in range(nsub):
        q0pos = (j * nsub + sb) * tq
        kstart = pl.multiple_of(jnp.clip(q0pos - DIL_SIDE, 0, m - tw), DIL_SIDE)
        rel = (kstart + lax.broadcasted_iota(jnp.int32, (tq, tw), 1)
               - q0pos - lax.broadcasted_iota(jnp.int32, (tq, tw), 0))
        kstarts.append(kstart)
        dists.append(jnp.abs(rel).astype(F32))

    def score_fn(r, sb, h):
        cols = slice(h // 2 * LANES, (h // 2 + 1) * LANES)

        def fn():
            q = q_ref[r, sb * tq:(sb + 1) * tq, cols] * masks[h % 2]
            s = lax.dot_general(q, k_ref[r, pl.ds(kstarts[sb], tw), cols], _NT, preferred_element_type=F32)
            return jnp.where(dists[sb] <= float(DIL_SIDE), s - slopes[h] * dists[sb], NEG_INF)
        return fn

    def value_fn(r, sb, h):
        cols = slice(h // 2 * LANES, (h // 2 + 1) * LANES)
        return lambda: v_ref[r, pl.ds(kstarts[sb], tw), cols]

    heads = [(r, sb, h) for r, sb in units for h in range(DIL_HEADS)]
    pv, row_max, row_sum = _softmax_pv_phases([score_fn(*x) for x in heads], [value_fn(*x) for x in heads], s_ref)
    for n, (r, sb) in enumerate(units):
        for p in range(DIL_HEADS // 2):
            a, b = n * DIL_HEADS + 2 * p, n * DIL_HEADS + 2 * p + 1
            rows, cols = slice(sb * tq, (sb + 1) * tq), slice(p * LANES, (p + 1) * LANES)
            o_ref[r, rows, cols] = jnp.where(lane < HEAD_DIM, pv[a] * (1.0 / row_sum[a]), pv[b] * (1.0 / row_sum[b]))
            lse_ref[r, rows, cols] = jnp.where(lane < HEAD_DIM, row_max[a] + jnp.log2(row_sum[a]),
                                               row_max[b] + jnp.log2(row_sum[b]))


def _dilated_group(qkv, col0, b, l, g):
    dil = DIL_PAIRS[g][1]
    assert DIL_PAIRS[g][0] // (2 * dil) == DIL_SIDE
    m = l // dil
    tq = min(DIL_TQ, m)
    tw = min(tq + 2 * DIL_SIDE, m)
    nsub = min(DIL_UNITS, m // tq)
    nclass = min(DIL_UNITS // nsub, dil)
    slopes = tuple(s * dil * LOG2E for s in _dil_slopes()[g * DIL_HEADS:(g + 1) * DIL_HEADS])
    return pl.pallas_call(
        functools.partial(_dilated_kernel, m=m, tq=tq, tw=tw, nclass=nclass, nsub=nsub, slopes=slopes),
        out_shape=[jax.ShapeDtypeStruct((b, dil, m, C_W), F32)] * 2,
        grid=(b, dil // nclass, m // (nsub * tq)),
        in_specs=[
            pl.BlockSpec((None, nclass, nsub * tq, C_W), lambda bi, r, j: (bi, r, j, col0)),
            pl.BlockSpec((None, nclass, m, C_W), lambda bi, r, j: (bi, r, 0, col0 + 1)),
            pl.BlockSpec((None, nclass, m, C_W), lambda bi, r, j: (bi, r, 0, col0 + 2)),
        ],
        out_specs=[pl.BlockSpec((None, nclass, nsub * tq, C_W), lambda bi, r, j: (bi, r, j, 0))] * 2,
        scratch_shapes=[pltpu.VMEM((nclass * nsub * DIL_HEADS, tq, tw), F32)],
        compiler_params=_params("parallel", "parallel", "arbitrary"),
        name=f"dilated{g}",
    )(qkv, qkv, qkv)


def _merge_kernel(x_ref, ga_ref, gb_ref, o0_ref, l0_ref, o1_ref, l1_ref, o2_ref, l2_ref, zc_ref, gl_ref,
                  bg_ref, wa_ref, wb_ref, wc_ref, wo_ref, gf_ref, out_ref, *scratch, tm, final_norm):
    outs, lses = [o0_ref[0]], [l0_ref[0]]
    for g, refs in ((1, (o1_ref, l1_ref)), (2, (o2_ref, l2_ref))):
        dil = DIL_PAIRS[g][1]
        for src, dst in zip(refs, scratch[2 * (g - 1):2 * g]):
            for t in range(C_W // LANES):
                for r in range(dil):
                    dst[t, pl.ds(r, tm // dil, stride=dil), :] = src[r, :, t * LANES:(t + 1) * LANES]
        outs.append(jnp.concatenate([scratch[2 * (g - 1)][t] for t in range(C_W // LANES)], axis=-1))
        lses.append(jnp.concatenate([scratch[2 * (g - 1) + 1][t] for t in range(C_W // LANES)], axis=-1))
    mx = jnp.maximum(jnp.maximum(lses[0], lses[1]), lses[2])
    wts = [jnp.exp2(x - mx) for x in lses]
    den = wts[0] + wts[1] + wts[2]
    yc = (wts[0] * outs[0] + wts[1] * outs[1] + wts[2] * outs[2]) / den
    gc = (yc * _silu(zc_ref[...].astype(F32))).astype(BF16)
    branches = (jnp.dot(ga_ref[...], wa_ref[...], preferred_element_type=F32),
                jnp.dot(gb_ref[...], wb_ref[...], preferred_element_type=F32),
                jnp.dot(gc, wc_ref[...], preferred_element_type=F32))
    merged = None
    for n, pn in enumerate(branches):
        logit = gl_ref[:, n * D_MODEL:(n + 1) * D_MODEL].astype(F32) + bg_ref[n:n + 1, :]
        term = pn / (1.0 + jnp.exp(-logit))
        merged = term if merged is None else merged + term
    y = x_ref[...] + jnp.dot(merged.astype(BF16), wo_ref[...], preferred_element_type=F32)
    if final_norm:
        y = _rms_scale(y) * gf_ref[...]
    out_ref[...] = y


def _merge(x, ga, gb, dil, u, b_gate, wa, wb, wc, wo, g_final, b, l, final_norm):
    tm = min(MERGE_TM, l)
    nt = l // tm
    row = lambda w, col=0: pl.BlockSpec((tm, w), lambda bi, i: (bi * nt + i, col))
    full = lambda a: pl.BlockSpec(a.shape, lambda bi, i: (0, 0))
    cm = lambda g: pl.BlockSpec((None, DIL_PAIRS[g][1], tm // DIL_PAIRS[g][1], C_W), lambda bi, i: (bi, 0, i, 0))
    dil_specs = [cm(g) for g in range(DIL_GROUPS) for _ in range(2)]
    dil_args = [a for pair in dil for a in pair]
    return pl.pallas_call(
        functools.partial(_merge_kernel, tm=tm, final_norm=final_norm),
        out_shape=jax.ShapeDtypeStruct((b * l, D_MODEL), F32),
        grid=(b, nt),
        in_specs=[row(D_MODEL), row(A_W), row(B_W)] + dil_specs
                 + [row(C_W, OFF_ZC // C_W), row(GL_W, OFF_GL // GL_W)]
                 + [full(a) for a in (b_gate, wa, wb, wc, wo, g_final)],
        out_specs=row(D_MODEL),
        scratch_shapes=[pltpu.VMEM((C_W // LANES, tm, LANES), F32)] * 4,
        compiler_params=_params("parallel", "parallel"),
        name="merge",
    )(x, ga, gb, *dil_args, u, u, b_gate, wa, wb, wc, wo, g_final)


def _trunk(x, layers, g_final):
    b, l, _ = x.shape
    x = x.reshape(b * l, D_MODEL)
    for n, p in enumerate(layers):
        lam_init = 0.8 - 0.6 * math.exp(-0.3 * n)
        u = _inproj(x, p["g_norm"], p["w_main"])
        c1, c2 = _inproj_dil(x, p["g_norm"], p["w_dil"], b, l)
        ga = _natten(u, p["na_bias"], b, l)
        gb = _diff(u, p["lam_qk"], p["g_diff"], b, l, lam_init)
        dil = [_dilated_group(u.reshape(b, 1, l, MAIN_W), OFF_C0 // C_W, b, l, 0),
               _dilated_group(c1, 0, b, l, 1),
               _dilated_group(c2, 0, b, l, 2)]
        x = _merge(x, ga, gb, dil, u, p["b_gate"], p["w_br_a"], p["w_br_b"], p["w_br_c"], p["w_out"],
                   g_final, b, l, final_norm=(n == len(layers) - 1))
    return x.reshape(b, l, D_MODEL)


def _prepare_layers(g_norm, w_in, b_gate, rpb, lam_qk, g_diff, w_br_a, w_br_b, w_br_c, w_out):
    segments = _column_segments()
    layers = []
    for n in range(w_in.shape[0]):
        w = jnp.concatenate([w_in[n][:, s:s + width] * scale for s, width, scale in segments], axis=1).astype(BF16)
        layers.append(dict(
            g_norm=g_norm[n].reshape(1, D_MODEL).astype(F32),
            w_main=w[:, :MAIN_W],
            w_dil=w[:, MAIN_W:],
            b_gate=b_gate[n].astype(F32),
            na_bias=_natten_bias(rpb[n]),
            lam_qk=lam_qk[n].astype(F32),
            g_diff=g_diff[n].reshape(1, 2 * HEAD_DIM).astype(F32),
            w_br_a=w_br_a[n].astype(BF16),
            w_br_b=w_br_b[n].astype(BF16),
            w_br_c=w_br_c[n].astype(BF16),
            w_out=w_out[n].astype(BF16),
        ))
    return layers


def kernel(x_prompt, x_sample, g_norm, w_in, b_gate, rpb, lam_qk, g_diff, w_br_a, w_br_b, w_br_c, w_out, g_final):
    layers = _prepare_layers(g_norm, w_in, b_gate, rpb, lam_qk, g_diff, w_br_a, w_br_b, w_br_c, w_out)
    gf = g_final.reshape(1, D_MODEL).astype(F32)
    return (_trunk(x_prompt, layers, gf), _trunk(x_sample, layers, gf))
```

```python
import functools
import math

import numpy as np
import jax
import jax.numpy as jnp
from jax import lax
from jax.experimental import pallas as pl
from jax.experimental.pallas import tpu as pltpu

F32 = jnp.float32
BF16 = jnp.bfloat16

D_MODEL = 1024
GRID_W = 64
HEAD_DIM = 64
RMS_EPS = 1e-6
NEG_INF = -1e30
LANES = 128

NA_HEADS = 8
NA_WIN_ROWS = 8
NA_WIN_COLS = 16
NA_Q_ROWS = 4
NA_K_ROWS = NA_Q_ROWS + NA_WIN_ROWS
NA_TQ = NA_Q_ROWS * GRID_W
NA_TK = NA_K_ROWS * GRID_W

DIFF_HEADS = 4
DIFF_TQ = 256
DIFF_TK = 512
DIFF_BLOCKS = 2

DIL_PAIRS = ((128, 1), (512, 4), (2048, 16))
DIL_GROUPS = 3
DIL_HEADS = 4
DIL_SIDE = 64
DIL_TQ = 256
DIL_UNITS = 4

A_W = NA_HEADS * HEAD_DIM
B_W = DIFF_HEADS * 2 * HEAD_DIM
C_QKV = DIL_GROUPS * DIL_HEADS * HEAD_DIM
C_W = DIL_HEADS * HEAD_DIM
N_BRANCH = 3
GL_W = N_BRANCH * D_MODEL
IN_WIDTH = 4 * A_W + 4 * B_W + 3 * C_QKV + C_W + GL_W

OFF_QA = 0
OFF_KA = OFF_QA + A_W
OFF_VA = OFF_KA + A_W
OFF_ZA = OFF_VA + A_W
OFF_QB = OFF_ZA + A_W
OFF_KB = OFF_QB + B_W
OFF_VB = OFF_KB + B_W
OFF_ZB = OFF_VB + B_W
OFF_ZC = OFF_ZB + B_W
OFF_C0 = OFF_ZC + C_W
MAIN_W = OFF_C0 + 3 * C_W
DIL_W = 3 * C_W
assert MAIN_W + 2 * DIL_W + GL_W == IN_WIDTH

VMEM_LIMIT = 56 * 1024 * 1024

IN_TM = 1024
IN_TN = MAIN_W // 2
MERGE_TM = 512

_NT = (((1,), (1,)), ((), ()))
LOG2E = math.log2(math.e)
Q_SCALE = HEAD_DIM ** -0.5 * LOG2E


def _column_segments():
    b0 = 4 * A_W
    c0 = b0 + 4 * B_W
    zc0 = c0 + 3 * C_QKV
    qkv = lambda g: [(c0 + n * C_QKV + g * C_W, C_W, Q_SCALE if n == 0 else 1.0) for n in range(3)]
    return ([(0, A_W, Q_SCALE), (A_W, 3 * A_W, 1.0), (b0, B_W, Q_SCALE), (b0 + B_W, 3 * B_W, 1.0),
             (zc0, C_W, 1.0)] + qkv(0) + qkv(1) + qkv(2) + [(zc0 + C_W, GL_W, 1.0)])


def _params(*sem):
    return pltpu.CompilerParams(dimension_semantics=sem, vmem_limit_bytes=VMEM_LIMIT)


def _silu(z):
    return z / (1.0 + jnp.exp(-z))


def _rms_scale(x):
    return x * lax.rsqrt(jnp.mean(x * x, axis=-1, keepdims=True) + RMS_EPS)


def _head_masks(scale):
    lane = lax.broadcasted_iota(jnp.int32, (1, LANES), 1)
    lo = jnp.where(lane < HEAD_DIM, scale, 0.0).astype(BF16)
    hi = jnp.where(lane >= HEAD_DIM, scale, 0.0).astype(BF16)
    return lo, hi


def _lane_tiles(x):
    if x.shape[-1] <= LANES:
        return [x]
    return [x[:, n * LANES:(n + 1) * LANES] for n in range(x.shape[-1] // LANES)]


def _inproj_kernel(x_ref, g_ref, w_ref, o_ref, h_ref):
    @pl.when(pl.program_id(1) == 0)
    def _():
        h_ref[...] = (_rms_scale(x_ref[...]) * g_ref[...]).astype(BF16)

    o_ref[...] = jnp.dot(h_ref[...], w_ref[...], preferred_element_type=F32).astype(BF16)


def _inproj(x, g, w):
    t = x.shape[0]
    tm = min(IN_TM, t)
    return pl.pallas_call(
        _inproj_kernel,
        out_shape=jax.ShapeDtypeStruct((t, MAIN_W), BF16),
        grid=(t // tm, MAIN_W // IN_TN),
        in_specs=[
            pl.BlockSpec((tm, D_MODEL), lambda i, j: (i, 0)),
            pl.BlockSpec((1, D_MODEL), lambda i, j: (0, 0)),
            pl.BlockSpec((D_MODEL, IN_TN), lambda i, j: (0, j)),
        ],
        out_specs=pl.BlockSpec((tm, IN_TN), lambda i, j: (i, j)),
        scratch_shapes=[pltpu.VMEM((tm, D_MODEL), BF16)],
        compiler_params=_params("parallel", "arbitrary"),
        name="inproj",
    )(x, g, w)


def _inproj_dil_kernel(x_ref, g_ref, w_ref, o1_ref, o2_ref, r_ref, *, tm):
    h = (_rms_scale(x_ref[...]) * g_ref[...]).astype(BF16)
    res = jnp.dot(h, w_ref[...], preferred_element_type=F32)
    for t, tile in enumerate(_lane_tiles(res)):
        r_ref[t] = tile
    tiles_per_group = DIL_W // LANES
    for g, o_ref in ((1, o1_ref), (2, o2_ref)):
        dil = DIL_PAIRS[g][1]
        for t in range(tiles_per_group):
            for r in range(dil):
                rows = r_ref[(g - 1) * tiles_per_group + t, pl.ds(r, tm // dil, stride=dil), :]
                o_ref[r, :, t * LANES:(t + 1) * LANES] = rows.astype(BF16)


def _inproj_dil(x, g, w, b, l):
    tm = min(IN_TM, l)
    nt = l // tm
    d1, d2 = DIL_PAIRS[1][1], DIL_PAIRS[2][1]
    return pl.pallas_call(
        functools.partial(_inproj_dil_kernel, tm=tm),
        out_shape=[jax.ShapeDtypeStruct((b, d1, l // d1, DIL_W), BF16),
                   jax.ShapeDtypeStruct((b, d2, l // d2, DIL_W), BF16)],
        grid=(b, nt),
        in_specs=[
            pl.BlockSpec((tm, D_MODEL), lambda bi, i: (bi * nt + i, 0)),
            pl.BlockSpec((1, D_MODEL), lambda bi, i: (0, 0)),
            pl.BlockSpec((D_MODEL, 2 * DIL_W), lambda bi, i: (0, 0)),
        ],
        out_specs=[pl.BlockSpec((None, d1, tm // d1, DIL_W), lambda bi, i: (bi, 0, i, 0)),
                   pl.BlockSpec((None, d2, tm // d2, DIL_W), lambda bi, i: (bi, 0, i, 0))],
        scratch_shapes=[pltpu.VMEM((2 * DIL_W // LANES, tm, LANES), F32)],
        compiler_params=_params("parallel", "parallel"),
        name="inproj_dil",
    )(x, g, w)


def _natten_bias(rpb):
    ri = np.arange(NA_Q_ROWS)[:, None]
    kr = np.arange(NA_K_ROWS)[None, :]
    c = np.arange(GRID_W)[:, None]
    kc = np.arange(GRID_W)[None, :]
    qstart = np.clip(c - NA_WIN_COLS // 2, 0, GRID_W - NA_WIN_COLS)
    col_ok = (kc >= qstart) & (kc < qstart + NA_WIN_COLS)
    col_idx = np.clip(kc - c + NA_WIN_COLS - 1, 0, 2 * NA_WIN_COLS - 2)
    col_sel = (col_idx[:, :, None] == np.arange(2 * NA_WIN_COLS - 1)) & col_ok[:, :, None]
    row_sel, row_ok = [], []
    for variant, q_minus_k_row0 in enumerate((0, NA_WIN_ROWS // 2, NA_WIN_ROWS)):
        first = (0 * ri, ri, 0 * ri + NA_K_ROWS - NA_WIN_ROWS)[variant]
        ok = (kr >= first) & (kr < first + NA_WIN_ROWS)
        idx = np.clip(kr - q_minus_k_row0 - ri + NA_WIN_ROWS - 1, 0, 2 * NA_WIN_ROWS - 2)
        row_sel.append((idx[:, :, None] == np.arange(2 * NA_WIN_ROWS - 1)) & ok[:, :, None])
        row_ok.append(ok)
    nrow = 2 * NA_WIN_ROWS - 1
    row_sel2 = np.stack(row_sel).reshape(3, NA_Q_ROWS, NA_K_ROWS // 2, 2, nrow)
    col_sel2 = np.zeros((GRID_W, 2, GRID_W, 2, 2 * NA_WIN_COLS - 1), bool)
    for e in range(2):
        col_sel2[:, e, :, e, :] = col_sel
    col_sel2 = col_sel2.reshape(GRID_W, LANES, 2, 2 * NA_WIN_COLS - 1)
    vals = jnp.einsum("vrkea,ham,cjem->vhrckj", jnp.asarray(row_sel2, F32), rpb.astype(F32),
                      jnp.asarray(col_sel2, F32), precision=lax.Precision.HIGHEST)
    ok = np.stack(row_ok)[:, None, :, None, :, None] & col_ok[None, None, None, :, None, :]
    ok = ok.reshape(3, 1, NA_Q_ROWS, GRID_W, NA_K_ROWS // 2, LANES)
    vals = jnp.where(jnp.asarray(ok), vals * LOG2E, NEG_INF)
    return vals.reshape(3, NA_HEADS, NA_TQ, NA_TK)


def _softmax_pv_phases(score_fns, value_fns, s_ref):
    heads = range(len(score_fns))
    lane_max = []
    for h in heads:
        s = score_fns[h]()
        s_ref[h] = s
        lane_max.append(functools.reduce(jnp.maximum, _lane_tiles(s)))
    row_max = [jnp.max(x, axis=-1, keepdims=True) for x in lane_max]
    pv, lane_sum = [], []
    for h in heads:
        e = jnp.exp2(s_ref[h] - row_max[h])
        lane_sum.append(functools.reduce(jnp.add, _lane_tiles(e)))
        pv.append(jnp.dot(e.astype(BF16), value_fns[h](), preferred_element_type=F32))
    row_sum = [jnp.sum(x, axis=-1, keepdims=True) for x in lane_sum]
    return pv, row_max, row_sum


def _natten_kernel(bias_ref, q_ref, k_ref, v_ref, z_ref, o_ref, s_ref, *, rows):
    i = pl.program_id(1)
    krow0 = jnp.clip(i * NA_Q_ROWS - NA_WIN_ROWS // 2, 0, rows - NA_K_ROWS)
    kstart = pl.multiple_of(krow0 * GRID_W, GRID_W)
    masks = _head_masks(1.0)
    lane = lax.broadcasted_iota(jnp.int32, (NA_TQ, LANES), 1)

    def score_fn(h):
        cols = slice(h // 2 * LANES, (h // 2 + 1) * LANES)
        return lambda: (lax.dot_general(q_ref[:, cols] * masks[h % 2], k_ref[pl.ds(kstart, NA_TK), cols], _NT,
                                        preferred_element_type=F32) + bias_ref[h])

    def value_fn(h):
        cols = slice(h // 2 * LANES, (h // 2 + 1) * LANES)
        return lambda: v_ref[pl.ds(kstart, NA_TK), cols]

    pv, _, row_sum = _softmax_pv_phases([score_fn(h) for h in range(NA_HEADS)],
                                        [value_fn(h) for h in range(NA_HEADS)], s_ref)
    for p in range(NA_HEADS // 2):
        cols = slice(p * LANES, (p + 1) * LANES)
        y = jnp.where(lane < HEAD_DIM, pv[2 * p] * (1.0 / row_sum[2 * p]), pv[2 * p + 1] * (1.0 / row_sum[2 * p + 1]))
        o_ref[:, cols] = (y * _silu(z_ref[:, cols].astype(F32))).astype(BF16)


def _natten(u, bias, b, l):
    rows = l // GRID_W
    nblk = l // NA_TQ
    assert rows >= NA_K_ROWS and nblk >= 3

    def bias_map(bi, i):
        return (jnp.where(i == 0, 0, jnp.where(i == nblk - 1, 2, 1)), 0, 0, 0)

    return pl.pallas_call(
        functools.partial(_natten_kernel, rows=rows),
        out_shape=jax.ShapeDtypeStruct((b * l, A_W), BF16),
        grid=(b, nblk),
        in_specs=[
            pl.BlockSpec((None, NA_HEADS, NA_TQ, NA_TK), bias_map),
            pl.BlockSpec((NA_TQ, A_W), lambda bi, i: (bi * nblk + i, OFF_QA // A_W)),
            pl.BlockSpec((l, A_W), lambda bi, i: (bi, OFF_KA // A_W)),
            pl.BlockSpec((l, A_W), lambda bi, i: (bi, OFF_VA // A_W)),
            pl.BlockSpec((NA_TQ, A_W), lambda bi, i: (bi * nblk + i, OFF_ZA // A_W)),
        ],
        out_specs=pl.BlockSpec((NA_TQ, A_W), lambda bi, i: (bi * nblk + i, 0)),
        scratch_shapes=[pltpu.VMEM((NA_HEADS, NA_TQ, NA_TK), F32)],
        compiler_params=_params("parallel", "arbitrary"),
        name="natten",
    )(bias, u, u, u, u)


def _diff_slopes():
    return [float(np.float32(2.0 ** (-8.0 * (i + 1) / DIFF_HEADS))) for i in range(DIFF_HEADS)]


def _diff_bias_table(l, tq, tk):
    nvar = (l - tk) // tq + (l - tq) // tq + 1
    n0 = (l - tk) // tq
    delta = (jnp.arange(nvar, dtype=jnp.int32) - n0) * tq
    q_minus_k = jnp.arange(tq, dtype=jnp.int32)[:, None] - jnp.arange(tk, dtype=jnp.int32)[None, :]
    dist = jnp.abs(q_minus_k[None] + delta[:, None, None]).astype(F32)
    slopes = jnp.asarray(_diff_slopes(), F32) * LOG2E
    return -slopes[:, None, None, None] * dist[None], n0


def _diff_kernel(lq_ref, gd_ref, bias_ref, q_ref, k_ref, v_ref, z_ref, o_ref, s_ref, st_ref, *,
                 l, tq, tk, nblk, n0, lam_init):
    i = pl.program_id(2)
    nck = l // tk
    ntile = tk // LANES
    lq = lq_ref[...]
    lam = (jnp.exp(jnp.sum(lq[0:1] * lq[1:2], axis=-1, keepdims=True))
           - jnp.exp(jnp.sum(lq[2:3] * lq[3:4], axis=-1, keepdims=True)) + lam_init)
    masks = _head_masks(1.0)
    blocks = range(nblk)

    for blk in blocks:
        q = q_ref[blk * tq:(blk + 1) * tq, :]
        for c in range(nck):
            k = k_ref[c * tk:(c + 1) * tk, :]
            bias = bias_ref[i * nblk + blk + (n0 - c * (tk // tq))]
            for n in range(2):
                s = lax.dot_general(q * masks[n], k, _NT, preferred_element_type=F32) + bias
                tiles = _lane_tiles(s)
                mloc = functools.reduce(jnp.maximum, tiles)
                es = [jnp.exp2(t - mloc) for t in tiles]
                for t in range(ntile):
                    s_ref[blk, n, c, :, t * LANES:(t + 1) * LANES] = es[t]
                st_ref[blk, 0, n, c] = mloc
                st_ref[blk, 1, n, c] = functools.reduce(jnp.add, es)

    for blk in blocks:
        coef = []
        for n in range(2):
            m = jnp.max(functools.reduce(jnp.maximum, [st_ref[blk, 0, n, c] for c in range(nck)]),
                        axis=-1, keepdims=True)
            fs = [jnp.exp2(st_ref[blk, 0, n, c] - m) for c in range(nck)]
            den = jnp.sum(functools.reduce(jnp.add, [fs[c] * st_ref[blk, 1, n, c] for c in range(nck)]),
                          axis=-1, keepdims=True)
            scale = (1.0 / den) if n == 0 else (lam / den)
            coef.append([f * scale for f in fs])

        o = None
        for c in range(nck):
            w = jnp.concatenate(
                [s_ref[blk, 0, c, :, t * LANES:(t + 1) * LANES] * coef[0][c]
                 - s_ref[blk, 1, c, :, t * LANES:(t + 1) * LANES] * coef[1][c] for t in range(ntile)], axis=-1)
            pv = jnp.dot(w.astype(BF16), v_ref[c * tk:(c + 1) * tk, :], preferred_element_type=F32)
            o = pv if o is None else o + pv

        y = (_rms_scale(o) * gd_ref[...]) * (1.0 - lam_init)
        rows = slice(blk * tq, (blk + 1) * tq)
        o_ref[rows, :] = (y * _silu(z_ref[rows, :].astype(F32))).astype(BF16)


def _diff(u, lam_qk, g_diff, b, l, lam_init):
    tq = min(DIFF_TQ, l)
    tk = min(DIFF_TK, l)
    nblk = min(DIFF_BLOCKS, l // tq)
    nq = l // (nblk * tq)
    qb, kb, vb, zb = (off // LANES for off in (OFF_QB, OFF_KB, OFF_VB, OFF_ZB))
    bias, n0 = _diff_bias_table(l, tq, tk)
    nvar = bias.shape[1]
    return pl.pallas_call(
        functools.partial(_diff_kernel, l=l, tq=tq, tk=tk, nblk=nblk, n0=n0, lam_init=lam_init),
        out_shape=jax.ShapeDtypeStruct((b * l, B_W), BF16),
        grid=(DIFF_HEADS, b, nq),
        in_specs=[
            pl.BlockSpec((4, HEAD_DIM), lambda h, bi, i: (0, 0)),
            pl.BlockSpec((1, 2 * HEAD_DIM), lambda h, bi, i: (0, 0)),
            pl.BlockSpec((None, nvar, tq, tk), lambda h, bi, i: (h, 0, 0, 0), pipeline_mode=pl.Buffered(1)),
            pl.BlockSpec((nblk * tq, LANES), lambda h, bi, i: (bi * nq + i, qb + h)),
            pl.BlockSpec((l, LANES), lambda h, bi, i: (bi, kb + h)),
            pl.BlockSpec((l, LANES), lambda h, bi, i: (bi, vb + h)),
            pl.BlockSpec((nblk * tq, LANES), lambda h, bi, i: (bi * nq + i, zb + h)),
        ],
        out_specs=pl.BlockSpec((nblk * tq, LANES), lambda h, bi, i: (bi * nq + i, h)),
        scratch_shapes=[pltpu.VMEM((nblk, 2, l // tk, tq, tk), F32),
                        pltpu.VMEM((nblk, 2, 2, l // tk, tq, LANES), F32)],
        compiler_params=_params("parallel", "parallel", "arbitrary"),
        name="diffattn",
    )(lam_qk, g_diff, bias, u, u, u, u)


def _dil_slopes():
    n = DIL_GROUPS * DIL_HEADS
    return [float(np.float32(2.0 ** (-8.0 * (i + 1) / n))) for i in range(n)]


def _dilated_kernel(q_ref, k_ref, v_ref, o_ref, lse_ref, s_ref, *, m, tq, tw, nclass, nsub, slopes):
    j = pl.program_id(2)
    masks = _head_masks(1.0)
    lane = lax.broadcasted_iota(jnp.int32, (tq, LANES), 1)
    units = [(r, sb) for r in range(nclass) for sb in range(nsub)]
    kstarts, dists = [], []
    for sb in range(nsub):
        q0pos = (j * nsub + sb) * tq
        kstart = pl.multiple_of(jnp.clip(q0pos - DIL_SIDE, 0, m - tw), DIL_SIDE)
        rel = (kstart + lax.broadcasted_iota(jnp.int32, (tq, tw), 1)
               - q0pos - lax.broadcasted_iota(jnp.int32, (tq, tw), 0))
        kstarts.append(kstart)
        dists.append(jnp.abs(rel).astype(F32))

    def score_fn(r, sb, h):
        cols = slice(h // 2 * LANES, (h // 2 + 1) * LANES)

        def fn():
            q = q_ref[r, sb * tq:(sb + 1) * tq, cols] * masks[h % 2]
            s = lax.dot_general(q, k_ref[r, pl.ds(kstarts[sb], tw), cols], _NT, preferred_element_type=F32)
            return jnp.where(dists[sb] <= float(DIL_SIDE), s - slopes[h] * dists[sb], NEG_INF)
        return fn

    def value_fn(r, sb, h):
        cols = slice(h // 2 * LANES, (h // 2 + 1) * LANES)
        return lambda: v_ref[r, pl.ds(kstarts[sb], tw), cols]

    heads = [(r, sb, h) for r, sb in units for h in range(DIL_HEADS)]
    pv, row_max, row_sum = _softmax_pv_phases([score_fn(*x) for x in heads], [value_fn(*x) for x in heads], s_ref)
    for n, (r, sb) in enumerate(units):
        for p in range(DIL_HEADS // 2):
            a, b = n * DIL_HEADS + 2 * p, n * DIL_HEADS + 2 * p + 1
            rows, cols = slice(sb * tq, (sb + 1) * tq), slice(p * LANES, (p + 1) * LANES)
            o_ref[r, rows, cols] = jnp.where(lane < HEAD_DIM, pv[a] * (1.0 / row_sum[a]), pv[b] * (1.0 / row_sum[b]))
            lse_ref[r, rows, cols] = jnp.where(lane < HEAD_DIM, row_max[a] + jnp.log2(row_sum[a]),
                                               row_max[b] + jnp.log2(row_sum[b]))


def _dilated_group(qkv, col0, b, l, g):
    dil = DIL_PAIRS[g][1]
    assert DIL_PAIRS[g][0] // (2 * dil) == DIL_SIDE
    m = l // dil
    tq = min(DIL_TQ, m)
    tw = min(tq + 2 * DIL_SIDE, m)
    nsub = min(DIL_UNITS, m // tq)
    nclass = min(DIL_UNITS // nsub, dil)
    slopes = tuple(s * dil * LOG2E for s in _dil_slopes()[g * DIL_HEADS:(g + 1) * DIL_HEADS])
    return pl.pallas_call(
        functools.partial(_dilated_kernel, m=m, tq=tq, tw=tw, nclass=nclass, nsub=nsub, slopes=slopes),
        out_shape=[jax.ShapeDtypeStruct((b, dil, m, C_W), F32)] * 2,
        grid=(b, dil // nclass, m // (nsub * tq)),
        in_specs=[
            pl.BlockSpec((None, nclass, nsub * tq, C_W), lambda bi, r, j: (bi, r, j, col0)),
            pl.BlockSpec((None, nclass, m, C_W), lambda bi, r, j: (bi, r, 0, col0 + 1)),
            pl.BlockSpec((None, nclass, m, C_W), lambda bi, r, j: (bi, r, 0, col0 + 2)),
        ],
        out_specs=[pl.BlockSpec((None, nclass, nsub * tq, C_W), lambda bi, r, j: (bi, r, j, 0))] * 2,
        scratch_shapes=[pltpu.VMEM((nclass * nsub * DIL_HEADS, tq, tw), F32)],
        compiler_params=_params("parallel", "parallel", "arbitrary"),
        name=f"dilated{g}",
    )(qkv, qkv, qkv)


def _merge_kernel(x_ref, ga_ref, gb_ref, o0_ref, l0_ref, o1_ref, l1_ref, o2_ref, l2_ref, zc_ref, gn_ref,
                  wg_ref, bg_ref, wa_ref, wb_ref, wc_ref, wo_ref, gf_ref, out_ref, *scratch, tm, final_norm):
    outs, lses = [o0_ref[0]], [l0_ref[0]]
    for g, refs in ((1, (o1_ref, l1_ref)), (2, (o2_ref, l2_ref))):
        dil = DIL_PAIRS[g][1]
        for src, dst in zip(refs, scratch[2 * (g - 1):2 * g]):
            for t in range(C_W // LANES):
                for r in range(dil):
                    dst[t, pl.ds(r, tm // dil, stride=dil), :] = src[r, :, t * LANES:(t + 1) * LANES]
        outs.append(jnp.concatenate([scratch[2 * (g - 1)][t] for t in range(C_W // LANES)], axis=-1))
        lses.append(jnp.concatenate([scratch[2 * (g - 1) + 1][t] for t in range(C_W // LANES)], axis=-1))
    mx = jnp.maximum(jnp.maximum(lses[0], lses[1]), lses[2])
    wts = [jnp.exp2(x - mx) for x in lses]
    den = wts[0] + wts[1] + wts[2]
    yc = (wts[0] * outs[0] + wts[1] * outs[1] + wts[2] * outs[2]) / den
    gc = (yc * _silu(zc_ref[...].astype(F32))).astype(BF16)
    branches = (jnp.dot(ga_ref[...], wa_ref[...], preferred_element_type=F32),
                jnp.dot(gb_ref[...], wb_ref[...], preferred_element_type=F32),
                jnp.dot(gc, wc_ref[...], preferred_element_type=F32))
    x = x_ref[...]
    h = (_rms_scale(x) * gn_ref[...]).astype(BF16)
    merged = None
    for n, pn in enumerate(branches):
        logit = jnp.dot(h, wg_ref[:, n * D_MODEL:(n + 1) * D_MODEL], preferred_element_type=F32) + bg_ref[n:n + 1, :]
        term = pn / (1.0 + jnp.exp(-logit))
        merged = term if merged is None else merged + term
    y = x + jnp.dot(merged.astype(BF16), wo_ref[...], preferred_element_type=F32)
    if final_norm:
        y = _rms_scale(y) * gf_ref[...]
    out_ref[...] = y


def _merge(x, ga, gb, dil, u, g_norm, w_gl, b_gate, wa, wb, wc, wo, g_final, b, l, final_norm):
    tm = min(MERGE_TM, l)
    nt = l // tm
    row = lambda w, col=0: pl.BlockSpec((tm, w), lambda bi, i: (bi * nt + i, col))
    full = lambda a: pl.BlockSpec(a.shape, lambda bi, i: (0, 0))
    cm = lambda g: pl.BlockSpec((None, DIL_PAIRS[g][1], tm // DIL_PAIRS[g][1], C_W), lambda bi, i: (bi, 0, i, 0))
    dil_specs = [cm(g) for g in range(DIL_GROUPS) for _ in range(2)]
    dil_args = [a for pair in dil for a in pair]
    return pl.pallas_call(
        functools.partial(_merge_kernel, tm=tm, final_norm=final_norm),
        out_shape=jax.ShapeDtypeStruct((b * l, D_MODEL), F32),
        grid=(b, nt),
        in_specs=[row(D_MODEL), row(A_W), row(B_W)] + dil_specs
                 + [row(C_W, OFF_ZC // C_W)]
                 + [full(a) for a in (g_norm, w_gl, b_gate, wa, wb, wc, wo, g_final)],
        out_specs=row(D_MODEL),
        scratch_shapes=[pltpu.VMEM((C_W // LANES, tm, LANES), F32)] * 4,
        compiler_params=_params("parallel", "parallel"),
        name="merge",
    )(x, ga, gb, *dil_args, u, g_norm, w_gl, b_gate, wa, wb, wc, wo, g_final)


def _trunk(x, layers, g_final):
    b, l, _ = x.shape
    x = x.reshape(b * l, D_MODEL)
    for n, p in enumerate(layers):
        lam_init = 0.8 - 0.6 * math.exp(-0.3 * n)
        u = _inproj(x, p["g_norm"], p["w_main"])
        c1, c2 = _inproj_dil(x, p["g_norm"], p["w_dil"], b, l)
        ga = _natten(u, p["na_bias"], b, l)
        gb = _diff(u, p["lam_qk"], p["g_diff"], b, l, lam_init)
        dil = [_dilated_group(u.reshape(b, 1, l, MAIN_W), OFF_C0 // C_W, b, l, 0),
               _dilated_group(c1, 0, b, l, 1),
               _dilated_group(c2, 0, b, l, 2)]
        x = _merge(x, ga, gb, dil, u, p["g_norm"], p["w_gl"], p["b_gate"], p["w_br_a"], p["w_br_b"], p["w_br_c"], p["w_out"],
                   g_final, b, l, final_norm=(n == len(layers) - 1))
    return x.reshape(b, l, D_MODEL)


def _prepare_layers(g_norm, w_in, b_gate, rpb, lam_qk, g_diff, w_br_a, w_br_b, w_br_c, w_out):
    segments = _column_segments()
    layers = []
    for n in range(w_in.shape[0]):
        w = jnp.concatenate([w_in[n][:, s:s + width] * scale for s, width, scale in segments], axis=1).astype(BF16)
        layers.append(dict(
            g_norm=g_norm[n].reshape(1, D_MODEL).astype(F32),
            w_main=w[:, :MAIN_W],
            w_dil=w[:, MAIN_W:MAIN_W + 2 * DIL_W],
            w_gl=w[:, MAIN_W + 2 * DIL_W:],
            b_gate=b_gate[n].astype(F32),
            na_bias=_natten_bias(rpb[n]),
            lam_qk=lam_qk[n].astype(F32),
            g_diff=g_diff[n].reshape(1, 2 * HEAD_DIM).astype(F32),
            w_br_a=w_br_a[n].astype(BF16),
            w_br_b=w_br_b[n].astype(BF16),
            w_br_c=w_br_c[n].astype(BF16),
            w_out=w_out[n].astype(BF16),
        ))
    return layers


def kernel(x_prompt, x_sample, g_norm, w_in, b_gate, rpb, lam_qk, g_diff, w_br_a, w_br_b, w_br_c, w_out, g_final):
    layers = _prepare_layers(g_norm, w_in, b_gate, rpb, lam_qk, g_diff, w_br_a, w_br_b, w_br_c, w_out)
    gf = g_final.reshape(1, D_MODEL).astype(F32)
    return (_trunk(x_prompt, layers, gf), _trunk(x_sample, layers, gf))
```
